```python
import math, functools
import jax, jax.numpy as jnp
from jax import lax
import numpy as np

D_MODEL = 1024
BATCH = 4
SEQ = 8192
DEPTH = 4

HEAD_DIM = 64
N_HEADS = 8
N_KV_GROUPS = 2
Q_W = N_HEADS * HEAD_DIM
KV_W = N_KV_GROUPS * HEAD_DIM
NSA_GATE_W = N_HEADS * 3
L_CMP = 32
CMP_STRIDE = 16
L_SEL = 64
N_TOPK = 16
W_WIN = 512
Q_BLK = 64
PHI_HIDDEN = 256
SGU_GROUPS = 8
SGU_HEAD = 64
SGU_W = SGU_GROUPS * SGU_HEAD
CHUNK = 128
D_FF = 2816
ROPE_THETA = 10000.0
EPS = 1e-6
NEG_INF = -1e30
FORCE = 1e9

IN_WIDTHS = (Q_W, KV_W, KV_W, KV_W, KV_W, KV_W, KV_W, NSA_GATE_W, 2 * SGU_W, D_MODEL, D_MODEL)
IN_SPLITS = tuple(int(s) for s in np.cumsum(IN_WIDTHS)[:-1])
IN_TOTAL = int(sum(IN_WIDTHS))

kernel_name = "hybrid_nsa_gmlp_macaron_trunk"


def rmsnorm(x, g):
    xf = x.astype(jnp.float32)
    y = xf * lax.rsqrt(jnp.mean(xf * xf, axis=-1, keepdims=True) + EPS)
    return (y * g.astype(jnp.float32)).astype(x.dtype)


def swiglu(h, w_gate_up, w_down):
    g, u = jnp.split(h @ w_gate_up, 2, axis=-1)
    return (jax.nn.silu(g) * u) @ w_down


def rope(x, pos):
    dh = x.shape[-1]
    inv = ROPE_THETA ** (-jnp.arange(0, dh, 2, dtype=jnp.float32) / dh)
    ang = pos.astype(jnp.float32)[:, None] * inv[None, :]
    cos = jnp.concatenate([jnp.cos(ang)] * 2, axis=-1)[:, None, :].astype(x.dtype)
    sin = jnp.concatenate([jnp.sin(ang)] * 2, axis=-1)[:, None, :].astype(x.dtype)
    x1, x2 = jnp.split(x, 2, axis=-1)
    return x * cos + jnp.concatenate([-x2, x1], axis=-1) * sin


def masked_softmax(s, mask):
    s = jnp.where(mask, s.astype(jnp.float32), NEG_INF)
    return jax.nn.softmax(s, axis=-1)


def nsa_attention(q, k_cmp, v_cmp, k_sel, v_sel, k_win, v_win, gates,
                  pos_k, pos_v, phik1, phik2, phiv1, phiv2):
    B, S, H, Dh = q.shape
    G = N_KV_GROUPS
    HPG = H // G
    scale = Dh ** -0.5
    n_cmp = (S - L_CMP) // CMP_STRIDE + 1
    n_sel = S // L_SEL
    n_topk = min(N_TOPK, n_sel)

    tok_idx = jnp.arange(n_cmp)[:, None] * CMP_STRIDE + jnp.arange(L_CMP)[None, :]

    def compress(kv, pe, w1, w2):
        blk = kv[:, tok_idx] + pe[None, None, :, None, :]
        flat = blk.transpose(0, 1, 3, 2, 4).reshape(B, n_cmp, G, L_CMP * Dh)
        return jax.nn.gelu(flat @ w1) @ w2

    cmp_start = jnp.arange(n_cmp) * CMP_STRIDE
    cmp_end = cmp_start + (L_CMP - 1)
    kc = rope(compress(k_cmp, pos_k, phik1, phik2), cmp_end)
    vc = compress(v_cmp, pos_v, phiv1, phiv2)

    sel_start = jnp.arange(n_sel) * L_SEL
    ov = jnp.minimum(cmp_start[:, None] + L_CMP, sel_start[None, :] + L_SEL) - \
        jnp.maximum(cmp_start[:, None], sel_start[None, :])
    sel_map = (jnp.clip(ov, 0) / L_CMP).astype(jnp.float32)

    ksb = k_sel.reshape(B, n_sel, L_SEL, G, Dh).transpose(0, 3, 1, 2, 4)
    vsb = v_sel.reshape(B, n_sel, L_SEL, G, Dh).transpose(0, 3, 1, 2, 4)
    kwp = jnp.pad(k_win, ((0, 0), (W_WIN, 0), (0, 0), (0, 0)))
    vwp = jnp.pad(v_win, ((0, 0), (W_WIN, 0), (0, 0), (0, 0)))
    bi = jnp.arange(B)[:, None, None, None]
    gi = jnp.arange(G)[None, :, None, None]
    blk_ids = jnp.arange(n_sel)

    def q_block(qb):
        q0 = qb * Q_BLK
        t = q0 + jnp.arange(Q_BLK)
        qh = lax.dynamic_slice_in_dim(q, q0, Q_BLK, axis=1).reshape(B, Q_BLK, G, HPG, Dh)
        gb = lax.dynamic_slice_in_dim(gates, q0, Q_BLK, axis=1).reshape(B, Q_BLK, G, HPG, 3)

        s = jnp.einsum('bqghd,bngd->bghqn', qh, kc) * scale
        valid = cmp_end[None, :] <= t[:, None]
        p_cmp = masked_softmax(s, valid) * jnp.any(valid, axis=-1)[:, None]
        o_cmp = jnp.einsum('bghqn,bngd->bqghd', p_cmp.astype(vc.dtype), vc)

        imp = jnp.einsum('bghqn,nj->bgqj', p_cmp, sel_map)
        cur = t // L_SEL
        imp = jnp.where(blk_ids[None, :] == cur[:, None], FORCE,
                        jnp.where(blk_ids[None, :] > cur[:, None], -FORCE, imp))
        _, idx = lax.top_k(imp, n_topk)
        ks = ksb[bi, gi, idx]
        vs = vsb[bi, gi, idx]
        kpos = idx[..., None] * L_SEL + jnp.arange(L_SEL)
        smask = (kpos <= t[:, None, None]).reshape(B, G, 1, Q_BLK, n_topk * L_SEL)
        s = jnp.einsum('bqghd,bgqkld->bghqkl', qh, ks) * scale
        p = masked_softmax(s.reshape(B, G, HPG, Q_BLK, n_topk * L_SEL), smask)
        o_sel = jnp.einsum('bghqx,bgqxd->bqghd', p.astype(vs.dtype),
                           vs.reshape(B, G, Q_BLK, n_topk * L_SEL, Dh))

        kw = lax.dynamic_slice_in_dim(kwp, q0, W_WIN + Q_BLK, axis=1)
        vw = lax.dynamic_slice_in_dim(vwp, q0, W_WIN + Q_BLK, axis=1)
        wpos = q0 - W_WIN + jnp.arange(W_WIN + Q_BLK)
        d = t[:, None] - wpos[None, :]
        wmask = (d >= 0) & (d < W_WIN) & (wpos[None, :] >= 0)
        s = jnp.einsum('bqghd,bkgd->bghqk', qh, kw) * scale
        p = masked_softmax(s, wmask)
        o_win = jnp.einsum('bghqk,bkgd->bqghd', p.astype(vw.dtype), vw)

        o = gb[..., 0:1] * o_cmp + gb[..., 1:2] * o_sel + gb[..., 2:3] * o_win
        return o.reshape(B, Q_BLK, H * Dh)

    out = lax.map(q_block, jnp.arange(S // Q_BLK))
    return out.transpose(1, 0, 2, 3).reshape(B, S, H * Dh)


def spatial_gating(uv, norm_g, w_s, b_s):
    u, v = jnp.split(jax.nn.gelu(uv), 2, axis=-1)
    v = rmsnorm(v, norm_g)
    B, S, _ = v.shape
    v = v.reshape(B, S // CHUNK, CHUNK, SGU_GROUPS, SGU_HEAD)
    ws = w_s * jnp.tril(jnp.ones((CHUNK, CHUNK), w_s.dtype))
    v = jnp.einsum('gts,bcsgd->bctgd', ws, v) + b_s.T[None, None, :, :, None]
    return u * v.reshape(B, S, SGU_W)


def setup_inputs(seed: int = 0) -> dict:
    key = jax.random.key(seed)
    ks = jax.random.split(key, 24)
    f32 = jnp.float32

    def nrm(k, shape, scale):
        return jax.random.normal(k, shape, f32) * scale

    def gain(k, shape):
        return 1.0 + 0.01 * jax.random.normal(k, shape, f32)

    L = DEPTH
    return {
        "x": jax.random.normal(ks[0], (BATCH, SEQ, D_MODEL), f32),
        "ffn1_norm": gain(ks[1], (L, D_MODEL)),
        "ffn1_w_gate_up": nrm(ks[2], (L, D_MODEL, 2 * D_FF), D_MODEL ** -0.5),
        "ffn1_w_down": nrm(ks[3], (L, D_FF, D_MODEL), D_FF ** -0.5),
        "mix_norm": gain(ks[4], (L, D_MODEL)),
        "w_in": nrm(ks[5], (L, D_MODEL, IN_TOTAL), D_MODEL ** -0.5),
        "cmp_pos_k": nrm(ks[6], (L, L_CMP, HEAD_DIM), 0.1),
        "cmp_pos_v": nrm(ks[7], (L, L_CMP, HEAD_DIM), 0.1),
        "phi_k_w1": nrm(ks[8], (L, L_CMP * HEAD_DIM, PHI_HIDDEN), (L_CMP * HEAD_DIM) ** -0.5),
        "phi_k_w2": nrm(ks[9], (L, PHI_HIDDEN, HEAD_DIM), PHI_HIDDEN ** -0.5),
        "phi_v_w1": nrm(ks[10], (L, L_CMP * HEAD_DIM, PHI_HIDDEN), (L_CMP * HEAD_DIM) ** -0.5),
        "phi_v_w2": nrm(ks[11], (L, PHI_HIDDEN, HEAD_DIM), PHI_HIDDEN ** -0.5),
        "sgu_norm": gain(ks[12], (L, SGU_W)),
        "sgu_w_s": nrm(ks[13], (L, SGU_GROUPS, CHUNK, CHUNK), CHUNK ** -0.5),
        "sgu_b_s": gain(ks[14], (L, SGU_GROUPS, CHUNK)),
        "proj_a": nrm(ks[15], (L, Q_W, D_MODEL), Q_W ** -0.5),
        "proj_b": nrm(ks[16], (L, SGU_W, D_MODEL), SGU_W ** -0.5),
        "w_out": nrm(ks[17], (L, D_MODEL, D_MODEL), D_MODEL ** -0.5),
        "ffn2_norm": gain(ks[18], (L, D_MODEL)),
        "ffn2_w_gate_up": nrm(ks[19], (L, D_MODEL, 2 * D_FF), D_MODEL ** -0.5),
        "ffn2_w_down": nrm(ks[20], (L, D_FF, D_MODEL), D_FF ** -0.5),
        "final_norm": gain(ks[21], (D_MODEL,)),
    }


def reference(x, ffn1_norm, ffn1_w_gate_up, ffn1_w_down, mix_norm, w_in,
              cmp_pos_k, cmp_pos_v, phi_k_w1, phi_k_w2, phi_v_w1, phi_v_w2,
              sgu_norm, sgu_w_s, sgu_b_s, proj_a, proj_b, w_out,
              ffn2_norm, ffn2_w_gate_up, ffn2_w_down, final_norm):
    B, S, _ = x.shape
    pos = jnp.arange(S)
    for l in range(DEPTH):
        x = x + 0.5 * swiglu(rmsnorm(x, ffn1_norm[l]), ffn1_w_gate_up[l], ffn1_w_down[l])

        h = rmsnorm(x, mix_norm[l])
        (q, kc, vc, ksel, vsel, kwin, vwin, g_nsa, uv, g_a, g_b) = \
            jnp.split(h @ w_in[l], IN_SPLITS, axis=-1)
        q = rope(q.reshape(B, S, N_HEADS, HEAD_DIM), pos)
        kv = lambda t: t.reshape(B, S, N_KV_GROUPS, HEAD_DIM)
        y_a = nsa_attention(
            q, kv(kc), kv(vc), rope(kv(ksel), pos), kv(vsel), rope(kv(kwin), pos), kv(vwin),
            jax.nn.sigmoid(g_nsa).reshape(B, S, N_HEADS, 3),
            cmp_pos_k[l], cmp_pos_v[l], phi_k_w1[l], phi_k_w2[l], phi_v_w1[l], phi_v_w2[l])
        y_b = spatial_gating(uv, sgu_norm[l], sgu_w_s[l], sgu_b_s[l])
        merged = jax.nn.sigmoid(g_a) * (y_a @ proj_a[l]) + jax.nn.sigmoid(g_b) * (y_b @ proj_b[l])
        x = x + merged @ w_out[l]

        x = x + 0.5 * swiglu(rmsnorm(x, ffn2_norm[l]), ffn2_w_gate_up[l], ffn2_w_down[l])
    return rmsnorm(x, final_norm)
```

```python
import functools

import jax
import jax.numpy as jnp
import numpy as np
from jax import lax
from jax.experimental import pallas as pl
from jax.experimental.pallas import tpu as pltpu

HEAD_DIM = 64
N_HEADS = 8
N_KV_GROUPS = 2
HEADS_PER_GROUP = N_HEADS // N_KV_GROUPS
Q_W = N_HEADS * HEAD_DIM
KV_W = N_KV_GROUPS * HEAD_DIM
NSA_GATE_W = N_HEADS * 3
L_CMP = 32
CMP_STRIDE = 16
L_SEL = 64
N_TOPK = 16
W_WIN = 512
PHI_HIDDEN = 256
SGU_GROUPS = 8
SGU_HEAD = 64
SGU_W = SGU_GROUPS * SGU_HEAD
CHUNK = 128
ROPE_THETA = 10000.0
EPS = 1e-6
NEG_INF = -1e30
FORCE = 1e9
SCALE = HEAD_DIM ** -0.5

LANES = 128
HALF = LANES // 2
VMEM_LIMIT = 56 * 1024 * 1024

TM = 512
TQ = 128
TK = 512
FF_CHUNK = 256

F32 = jnp.float32
BF16 = jnp.bfloat16


def _dot(a, b):
    return jnp.dot(a, b, preferred_element_type=F32)


def _dot_nt(a, b):
    return lax.dot_general(a, b, (((1,), (1,)), ((), ())), preferred_element_type=F32)


def _rms(x, g):
    return x * lax.rsqrt(jnp.mean(x * x, axis=-1, keepdims=True) + EPS) * g


def _gelu_tanh(x):
    c = np.float32(np.sqrt(2.0 / np.pi))
    return 0.5 * x * (1.0 + jnp.tanh(c * (x + 0.044715 * (x * x * x))))


def _lane_pick(x, col):
    lane = lax.broadcasted_iota(jnp.int32, x.shape, 1)
    return jnp.sum(jnp.where(lane == col, x, 0.0), axis=1, keepdims=True)


def _params(*sem):
    return pltpu.CompilerParams(dimension_semantics=sem, vmem_limit_bytes=VMEM_LIMIT)


def _ffn_body(x_ref, g_ref, wgu_ref, wd_ref, fg_ref, o_ref, *, d_ff, final_norm):
    x = x_ref[...]
    h = _rms(x, g_ref[...]).astype(BF16)
    acc = jnp.zeros(x.shape, F32)
    for c in range(d_ff // FF_CHUNK):
        lo = c * FF_CHUNK
        gate = _dot(h, wgu_ref[:, lo:lo + FF_CHUNK])
        up = _dot(h, wgu_ref[:, d_ff + lo:d_ff + lo + FF_CHUNK])
        act = (gate * jax.nn.sigmoid(gate) * up).astype(BF16)
        acc = acc + _dot(act, wd_ref[lo:lo + FF_CHUNK, :])
    y = x + 0.5 * acc
    if final_norm:
        y = _rms(y, fg_ref[...])
    o_ref[...] = y


def _ffn(x2, g, wgu, wd, fg, final_norm):
    t, d = x2.shape
    d_ff = wd.shape[0]
    assert t % TM == 0 and d_ff % FF_CHUNK == 0
    const = lambda i: (0, 0)
    return pl.pallas_call(
        functools.partial(_ffn_body, d_ff=d_ff, final_norm=final_norm),
        grid=(t // TM,),
        in_specs=[
            pl.BlockSpec((TM, d), lambda i: (i, 0)),
            pl.BlockSpec((1, d), const),
            pl.BlockSpec((d, 2 * d_ff), const, pipeline_mode=pl.Buffered(1)),
            pl.BlockSpec((d_ff, d), const, pipeline_mode=pl.Buffered(1)),
            pl.BlockSpec((1, d), const),
        ],
        out_specs=pl.BlockSpec((TM, d), lambda i: (i, 0)),
        out_shape=jax.ShapeDtypeStruct((t, d), F32),
        compiler_params=_params("parallel"),
        name="ffn",
    )(x2, g, wgu, wd, fg)


_OFF_Q, _OFF_KC, _OFF_VC, _OFF_KS, _OFF_VS, _OFF_KW, _OFF_VW = 0, 512, 640, 768, 896, 1024, 1152
_OFF_RQ, _OFF_RKS, _OFF_RKW, _OFF_GN, _OFF_UV = 1280, 1792, 1920, 2048, 2304
_IN_COLS = _OFF_UV + 2 * SGU_W


def _inproj_body(x_ref, g_ref, w_ref, cos_ref, sin_ref, sgn_ref, ws_ref, bs_ref,
                 q_ref, kc_ref, vc_ref, ksa_ref, ksb_ref, vs_ref, kwa_ref, kwb_ref, vw_ref,
                 gn_ref, yb_ref, *, tiles_per_seq):
    h = _rms(x_ref[...], g_ref[...]).astype(BF16)
    p = _dot(h, w_ref[...])
    cos = cos_ref[...]
    sin = sin_ref[...]
    cos4 = jnp.concatenate([cos] * (Q_W // LANES), axis=1)
    sin4 = jnp.concatenate([sin] * (Q_W // LANES), axis=1)

    q = p[:, _OFF_Q:_OFF_Q + Q_W] * cos4 + p[:, _OFF_RQ:_OFF_RQ + Q_W] * sin4
    q_ref[...] = (q * SCALE).astype(BF16)
    kc_ref[...] = p[:, _OFF_KC:_OFF_KC + KV_W]
    vc_ref[...] = p[:, _OFF_VC:_OFF_VC + KV_W]

    ksel = p[:, _OFF_KS:_OFF_KS + KV_W] * cos + p[:, _OFF_RKS:_OFF_RKS + KV_W] * sin
    kwin = p[:, _OFF_KW:_OFF_KW + KV_W] * cos + p[:, _OFF_RKW:_OFF_RKW + KV_W] * sin
    vsel = p[:, _OFF_VS:_OFF_VS + KV_W]
    vwin = p[:, _OFF_VW:_OFF_VW + KV_W]

    shape = ksel.shape
    lane = lax.broadcasted_iota(jnp.int32, shape, 1)
    row = lax.broadcasted_iota(jnp.int32, shape, 0)
    lo = lane < HALF
    pos = (pl.program_id(0) % tiles_per_seq) * TM + row
    blk = jnp.bitwise_and(jnp.right_shift(pos, 6), HALF - 1)
    oh_hi = jnp.where(lane == blk + HALF, 1.0, 0.0)
    oh_lo = jnp.where(lane == blk, 1.0, 0.0)

    ksel_sw = pltpu.roll(ksel, HALF, 1)
    kwin_sw = pltpu.roll(kwin, HALF, 1)
    vsel_sw = pltpu.roll(vsel, HALF, 1)
    vwin_sw = pltpu.roll(vwin, HALF, 1)
    for g, (ks_lo, ks_hi, kw_lo, kw_hi, v_s, v_s_sw, v_w, v_w_sw) in enumerate((
            (ksel, ksel_sw, kwin, kwin_sw, vsel, vsel_sw, vwin, vwin_sw),
            (ksel_sw, ksel, kwin_sw, kwin, vsel_sw, vsel, vwin_sw, vwin))):
        ksa_ref[g] = jnp.where(lo, ks_lo, oh_hi).astype(BF16)
        ksb_ref[g] = jnp.where(lo, oh_lo, ks_hi).astype(BF16)
        kwa_ref[g] = jnp.where(lo, kw_lo, 0.0).astype(BF16)
        kwb_ref[g] = jnp.where(lo, 0.0, kw_hi).astype(BF16)
        vs_ref[g] = jnp.where(lo, v_s, v_s_sw).astype(BF16)
        vw_ref[g] = jnp.where(lo, v_w, v_w_sw).astype(BF16)

    gn_ref[...] = jax.nn.sigmoid(p[:, _OFF_GN:_OFF_GN + 2 * LANES])

    ge = _gelu_tanh(p[:, _OFF_UV:_OFF_UV + 2 * SGU_W])
    u = ge[:, :SGU_W]
    v = _rms(ge[:, SGU_W:], sgn_ref[...])
    tri = (lax.broadcasted_iota(jnp.int32, (CHUNK, CHUNK), 0)
           >= lax.broadcasted_iota(jnp.int32, (CHUNK, CHUNK), 1))
    lo_c = lax.broadcasted_iota(jnp.int32, (CHUNK, LANES), 1) < HALF
    n_chunks = TM // CHUNK
    cols = []
    for pr in range(SGU_W // LANES):
        w_even = jnp.where(tri, ws_ref[2 * pr], 0.0).astype(BF16)
        w_odd = jnp.where(tri, ws_ref[2 * pr + 1], 0.0).astype(BF16)
        rows = []
        for r in range(n_chunks):
            vp = v[r * CHUNK:(r + 1) * CHUNK, pr * LANES:(pr + 1) * LANES]
            v_even = jnp.where(lo_c, vp, 0.0).astype(BF16)
            v_odd = jnp.where(lo_c, 0.0, vp).astype(BF16)
            rows.append(_dot(w_even, v_even) + _dot(w_odd, v_odd))
        cols.append(jnp.concatenate(rows, axis=0))
    vo = jnp.concatenate(cols, axis=1)
    bias = jnp.concatenate([bs_ref[...]] * n_chunks, axis=0)
    yb_ref[...] = (u * (vo + bias)).astype(BF16)


def _inproj(x2, g, w, cos, sin, sgn, ws, bs, seq):
    t, d = x2.shape
    tiles_per_seq = seq // TM
    const2 = lambda i: (0, 0)
    row = lambda i: (i, 0)
    grp = lambda i: (0, i, 0)
    pos = lambda i: (i % tiles_per_seq, 0)
    kv_shape = jax.ShapeDtypeStruct((N_KV_GROUPS, t, LANES), BF16)
    kv_spec = pl.BlockSpec((N_KV_GROUPS, TM, LANES), grp)
    return pl.pallas_call(
        functools.partial(_inproj_body, tiles_per_seq=tiles_per_seq),
        grid=(t // TM,),
        in_specs=[
            pl.BlockSpec((TM, d), row),
            pl.BlockSpec((1, d), const2),
            pl.BlockSpec((d, _IN_COLS), const2, pipeline_mode=pl.Buffered(1)),
            pl.BlockSpec((TM, LANES), pos),
            pl.BlockSpec((TM, LANES), pos),
            pl.BlockSpec((1, SGU_W), const2),
            pl.BlockSpec((SGU_GROUPS, CHUNK, CHUNK), lambda i: (0, 0, 0)),
            pl.BlockSpec((CHUNK, SGU_W), const2),
        ],
        out_specs=[
            pl.BlockSpec((TM, Q_W), row),
            pl.BlockSpec((TM, KV_W), row),
            pl.BlockSpec((TM, KV_W), row),
            kv_spec, kv_spec, kv_spec, kv_spec, kv_spec, kv_spec,
            pl.BlockSpec((TM, 2 * LANES), row),
            pl.BlockSpec((TM, SGU_W), row),
        ],
        out_shape=[
            jax.ShapeDtypeStruct((t, Q_W), BF16),
            jax.ShapeDtypeStruct((t, KV_W), F32),
            jax.ShapeDtypeStruct((t, KV_W), F32),
            kv_shape, kv_shape, kv_shape, kv_shape, kv_shape, kv_shape,
            jax.ShapeDtypeStruct((t, 2 * LANES), F32),
            jax.ShapeDtypeStruct((t, SGU_W), BF16),
        ],
        compiler_params=_params("parallel"),
        name="inproj",
    )(x2, g, w, cos, sin, sgn, ws, bs)


def _compress_body(kc_ref, vc_ref, pekt_ref, pekb_ref, pevt_ref, pevb_ref,
                   wkt_ref, wkb_ref, wvt_ref, wvb_ref, w2k_ref, w2v_ref, cos_ref, sin_ref,
                   kca_ref, kcb_ref, vc2_ref):
    n_rows = kc_ref.shape[0]

    def hidden(c_ref, pet_ref, peb_ref, wt_ref, wb_ref):
        c = c_ref[...]
        top = _dot((c + pet_ref[...]).astype(BF16), wt_ref[...])
        bot = _dot((c + peb_ref[...]).astype(BF16), wb_ref[...])
        return _gelu_tanh(top + pltpu.roll(bot, n_rows - 1, 0))

    hk = hidden(kc_ref, pekt_ref, pekb_ref, wkt_ref, wkb_ref)
    hv = hidden(vc_ref, pevt_ref, pevb_ref, wvt_ref, wvb_ref)
    cos = cos_ref[...]
    sin = sin_ref[...]
    for g in range(N_KV_GROUPS):
        hkg = hk[:, g * PHI_HIDDEN:(g + 1) * PHI_HIDDEN].astype(BF16)
        hvg = hv[:, g * PHI_HIDDEN:(g + 1) * PHI_HIDDEN].astype(BF16)
        kk = _dot(hkg, w2k_ref[...])
        kca_ref[g] = (kk[:, 0:LANES] * cos + kk[:, 2 * LANES:3 * LANES] * sin).astype(BF16)
        kcb_ref[g] = (kk[:, LANES:2 * LANES] * cos + kk[:, 3 * LANES:4 * LANES] * sin).astype(BF16)
        vc2_ref[g] = _dot(hvg, w2v_ref[...]).astype(BF16)


def _compress(kc_rows, vc_rows, pe, w1, w2k, w2v, cos_c, sin_c, batch):
    total, width = kc_rows.shape
    n_rows = total // batch
    const = lambda b: (0, 0)
    out_shape = jax.ShapeDtypeStruct((N_KV_GROUPS, total, LANES), BF16)
    out_spec = pl.BlockSpec((N_KV_GROUPS, n_rows, LANES), lambda b: (0, b, 0))
    pe_spec = pl.BlockSpec((1, width), const)
    w1_spec = pl.BlockSpec((width, N_KV_GROUPS * PHI_HIDDEN), const)
    return pl.pallas_call(
        _compress_body,
        grid=(batch,),
        in_specs=[
            pl.BlockSpec((n_rows, width), lambda b: (b, 0)),
            pl.BlockSpec((n_rows, width), lambda b: (b, 0)),
            pe_spec, pe_spec, pe_spec, pe_spec,
            w1_spec, w1_spec, w1_spec, w1_spec,
            pl.BlockSpec((PHI_HIDDEN, 4 * LANES), const),
            pl.BlockSpec((PHI_HIDDEN, LANES), const),
            pl.BlockSpec((n_rows, LANES), const),
            pl.BlockSpec((n_rows, LANES), const),
        ],
        out_specs=[out_spec, out_spec, out_spec],
        out_shape=[out_shape, out_shape, out_shape],
        compiler_params=_params("parallel"),
        name="compress",
    )(kc_rows, vc_rows, *pe, *w1, w2k, w2v, cos_c, sin_c)


def _cmp_body(q_ref, kca_ref, kcb_ref, vc_ref, gn_ref, selt_ref, o_ref, bias_ref):
    n_cmp = kca_ref.shape[1]
    t0 = pl.program_id(2) * TQ
    tok = t0 + lax.broadcasted_iota(jnp.int32, (TQ, n_cmp), 0)
    cmp_end = lax.broadcasted_iota(jnp.int32, (TQ, n_cmp), 1) * CMP_STRIDE + (L_CMP - 1)
    valid = cmp_end <= tok
    any_valid = jnp.where(t0 + lax.broadcasted_iota(jnp.int32, (TQ, 1), 0) >= L_CMP - 1, 1.0, 0.0)
    lo = lax.broadcasted_iota(jnp.int32, (TQ, LANES), 1) < HALF
    gn = gn_ref[...]
    vc = vc_ref[0]

    psum = jnp.zeros((TQ, n_cmp), F32)
    for pr in range(HEADS_PER_GROUP // 2):
        qp = q_ref[:, pr * LANES:(pr + 1) * LANES]
        outs = []
        for e, k_ref in enumerate((kca_ref, kcb_ref)):
            s = jnp.where(valid, _dot_nt(qp, k_ref[0]), NEG_INF)
            ex = jnp.exp(s - jnp.max(s, axis=1, keepdims=True))
            p = ex / jnp.sum(ex, axis=1, keepdims=True) * any_valid
            psum = psum + p
            gate = _lane_pick(gn, 0 * HEADS_PER_GROUP + 2 * pr + e)
            outs.append(_dot(p.astype(BF16), vc) * gate)
        o_ref[:, pr * LANES:(pr + 1) * LANES] = jnp.where(lo, outs[0], outs[1])

    selt = selt_ref[...]
    p_hi = psum.astype(BF16)
    r1 = psum - p_hi.astype(F32)
    p_mid = r1.astype(BF16)
    p_lo = (r1 - p_mid.astype(F32)).astype(BF16)
    imp = _dot_nt(selt, p_hi) + _dot_nt(selt, p_mid) + _dot_nt(selt, p_lo)

    n_blk = imp.shape[0]
    jj = lax.broadcasted_iota(jnp.int32, (n_blk, TQ), 0)
    cur = jnp.right_shift(t0 + lax.broadcasted_iota(jnp.int32, (n_blk, TQ), 1), 6)
    jf = jj.astype(F32)
    val = jnp.where(jj == cur, FORCE, jnp.where(jj > cur, -FORCE, imp))
    sel = jnp.zeros((n_blk, TQ), F32)
    for _ in range(N_TOPK):
        best = jnp.max(val, axis=0, keepdims=True)
        first = jnp.min(jnp.where(val == best, jf, 1e6), axis=0, keepdims=True)
        hit = jf == first
        sel = jnp.where(hit, 1.0, sel)
        val = jnp.where(hit, -3e38, val)
    bias_t = jnp.where(jnp.logical_and(sel > 0.5, jj <= cur), 0.0, NEG_INF)
    bias_ref[0] = bias_t.T.astype(BF16)


def _cmp_attention(q, kca, kcb, vc2, gn, selt, batch, seq):
    t = q.shape[0]
    n_cmp = kca.shape[1] // batch
    nq = seq // TQ
    kv_spec = pl.BlockSpec((1, n_cmp, LANES), lambda b, g, i: (g, b, 0))
    return pl.pallas_call(
        _cmp_body,
        grid=(batch, N_KV_GROUPS, nq),
        in_specs=[
            pl.BlockSpec((TQ, 2 * LANES), lambda b, g, i: (b * nq + i, g)),
            kv_spec, kv_spec, kv_spec,
            pl.BlockSpec((TQ, LANES), lambda b, g, i: (b * nq + i, g)),
            pl.BlockSpec(selt.shape, lambda b, g, i: (0, 0)),
        ],
        out_specs=[
            pl.BlockSpec((TQ, 2 * LANES), lambda b, g, i: (b * nq + i, g)),
            pl.BlockSpec((1, TQ, LANES), lambda b, g, i: (g, b * nq + i, 0)),
        ],
        out_shape=[
            jax.ShapeDtypeStruct((t, Q_W), F32),
            jax.ShapeDtypeStruct((N_KV_GROUPS, t, LANES), BF16),
        ],
        compiler_params=_params("parallel", "parallel", "parallel"),
        name="cmp_attn",
    )(q, kca, kcb, vc2, gn, selt)


def _sel_body(q_ref, bias_ref, ka_ref, kb_ref, v_ref, gn_ref, o_ref, m_ref, l_ref, acc_ref):
    t0 = pl.program_id(2) * TQ
    lo = lax.broadcasted_iota(jnp.int32, (TQ, LANES), 1) < HALF
    bias = bias_ref[0].astype(F32)
    bias_sw = pltpu.roll(bias, HALF, 1)
    qp = [q_ref[:, pr * LANES:(pr + 1) * LANES].astype(F32) for pr in range(HEADS_PER_GROUP // 2)]

    m_ref[...] = jnp.full(m_ref.shape, -jnp.inf, F32)
    l_ref[...] = jnp.zeros(l_ref.shape, F32)
    acc_ref[...] = jnp.zeros(acc_ref.shape, F32)

    n_tiles = (t0 + TQ - 1) // TK + 1
    tiles_per_half = (HALF * L_SEL) // TK
    rows = HEADS_PER_GROUP * TQ
    tok = t0 + lax.broadcasted_iota(jnp.int32, (rows, TK), 0) % TQ
    col = lax.broadcasted_iota(jnp.int32, (rows, TK), 1)

    def run_half(b_even, b_odd, first, last):
        lhs_a = jnp.concatenate([jnp.where(lo, x, b_even) for x in qp], axis=0).astype(BF16)
        lhs_b = jnp.concatenate([jnp.where(lo, b_odd, x) for x in qp], axis=0).astype(BF16)

        def step(j, carry):
            k0 = pl.multiple_of(j * TK, TK)
            s = jnp.concatenate([_dot_nt(lhs_a, ka_ref[0, pl.ds(k0, TK), :]),
                                 _dot_nt(lhs_b, kb_ref[0, pl.ds(k0, TK), :])], axis=0)
            s = jnp.where(k0 + col <= tok, s, NEG_INF)
            m_old = m_ref[...]
            m_new = jnp.maximum(m_old, jnp.max(s, axis=1, keepdims=True))
            alpha = jnp.exp(m_old - m_new)
            p = jnp.exp(s - jnp.concatenate([m_new] * (TK // LANES), axis=1))
            l_ref[...] = alpha * l_ref[...] + jnp.sum(p, axis=1, keepdims=True)
            acc_ref[...] = alpha * acc_ref[...] + _dot(p.astype(BF16), v_ref[0, pl.ds(k0, TK), :])
            m_ref[...] = m_new
            return carry

        lax.fori_loop(first, last, step, 0)

    run_half(bias_sw, bias, 0, jnp.minimum(n_tiles, tiles_per_half))
    run_half(bias, bias_sw, tiles_per_half, jnp.maximum(n_tiles, tiles_per_half))

    gn = gn_ref[...]
    o = acc_ref[...] / l_ref[...]
    for pr in range(HEADS_PER_GROUP // 2):
        even = o[pr * TQ:(pr + 1) * TQ] * _lane_pick(gn, 1 * HEADS_PER_GROUP + 2 * pr)
        odd = o[(2 + pr) * TQ:(3 + pr) * TQ] * _lane_pick(gn, 1 * HEADS_PER_GROUP + 2 * pr + 1)
        o_ref[:, pr * LANES:(pr + 1) * LANES] = jnp.where(lo, even, odd)


def _sel_attention(q, bias, ksa, ksb, vs2, gn, batch, seq):
    t = q.shape[0]
    nq = seq // TQ
    kv_spec = pl.BlockSpec((1, seq, LANES), lambda b, g, i: (g, b, 0))
    rows = HEADS_PER_GROUP * TQ
    return pl.pallas_call(
        _sel_body,
        grid=(batch, N_KV_GROUPS, nq),
        in_specs=[
            pl.BlockSpec((TQ, 2 * LANES), lambda b, g, i: (b * nq + i, g)),
            pl.BlockSpec((1, TQ, LANES), lambda b, g, i: (g, b * nq + i, 0)),
            kv_spec, kv_spec, kv_spec,
            pl.BlockSpec((TQ, LANES), lambda b, g, i: (b * nq + i, g)),
        ],
        out_specs=pl.BlockSpec((TQ, 2 * LANES), lambda b, g, i: (b * nq + i, g)),
        out_shape=jax.ShapeDtypeStruct((t, Q_W), F32),
        scratch_shapes=[pltpu.VMEM((rows, LANES), F32)] * 3,
        compiler_params=_params("parallel", "parallel", "arbitrary"),
        name="sel_attn",
    )(q, bias, ksa, ksb, vs2, gn)


_WIN_KEYS = W_WIN + TQ


def _win_body(q_ref, ka_ref, kb_ref, v_ref, gn_ref, o_ref):
    t0 = pl.program_id(2) * TQ
    k0 = pl.multiple_of(jnp.maximum(t0 - W_WIN, 0), TQ)
    tok = t0 + lax.broadcasted_iota(jnp.int32, (TQ, _WIN_KEYS), 0)
    key = k0 + lax.broadcasted_iota(jnp.int32, (TQ, _WIN_KEYS), 1)
    dist = tok - key
    valid = jnp.logical_and(dist >= 0, dist < W_WIN)
    lo = lax.broadcasted_iota(jnp.int32, (TQ, LANES), 1) < HALF
    gn = gn_ref[...]
    v = v_ref[0, pl.ds(k0, _WIN_KEYS), :]
    for pr in range(HEADS_PER_GROUP // 2):
        qp = q_ref[:, pr * LANES:(pr + 1) * LANES]
        outs = []
        for e, k_ref in enumerate((ka_ref, kb_ref)):
            s = jnp.where(valid, _dot_nt(qp, k_ref[0, pl.ds(k0, _WIN_KEYS), :]), NEG_INF)
            ex = jnp.exp(s - jnp.max(s, axis=1, keepdims=True))
            p = ex / jnp.sum(ex, axis=1, keepdims=True)
            gate = _lane_pick(gn, 2 * HEADS_PER_GROUP + 2 * pr + e)
            outs.append(_dot(p.astype(BF16), v) * gate)
        o_ref[:, pr * LANES:(pr + 1) * LANES] = jnp.where(lo, outs[0], outs[1])


def _win_attention(q, kwa, kwb, vw2, gn, batch, seq):
    t = q.shape[0]
    nq = seq // TQ
    kv_spec = pl.BlockSpec((1, seq, LANES), lambda b, g, i: (g, b, 0))
    return pl.pallas_call(
        _win_body,
        grid=(batch, N_KV_GROUPS, nq),
        in_specs=[
            pl.BlockSpec((TQ, 2 * LANES), lambda b, g, i: (b * nq + i, g)),
            kv_spec, kv_spec, kv_spec,
            pl.BlockSpec((TQ, LANES), lambda b, g, i: (b * nq + i, g)),
        ],
        out_specs=pl.BlockSpec((TQ, 2 * LANES), lambda b, g, i: (b * nq + i, g)),
        out_shape=jax.ShapeDtypeStruct((t, Q_W), F32),
        compiler_params=_params("parallel", "parallel", "parallel"),
        name="win_attn",
    )(q, kwa, kwb, vw2, gn)


def _merge_body(x_ref, g_ref, wg_ref, oc_ref, os_ref, ow_ref, yb_ref, pa_ref, pb_ref, wo_ref, o_ref):
    x = x_ref[...]
    d = x.shape[1]
    h = _rms(x, g_ref[...]).astype(BF16)
    gates = jax.nn.sigmoid(_dot(h, wg_ref[...]))
    ya = (oc_ref[...] + os_ref[...] + ow_ref[...]).astype(BF16)
    merged = gates[:, :d] * _dot(ya, pa_ref[...]) + gates[:, d:] * _dot(yb_ref[...], pb_ref[...])
    o_ref[...] = x + _dot(merged.astype(BF16), wo_ref[...])


def _merge(x2, g, wg, oc, osel, ow, yb, pa, pb, wo):
    t, d = x2.shape
    const = lambda i: (0, 0)
    row = lambda i: (i, 0)
    return pl.pallas_call(
        _merge_body,
        grid=(t // TM,),
        in_specs=[
            pl.BlockSpec((TM, d), row),
            pl.BlockSpec((1, d), const),
            pl.BlockSpec((d, 2 * d), const),
            pl.BlockSpec((TM, Q_W), row),
            pl.BlockSpec((TM, Q_W), row),
            pl.BlockSpec((TM, Q_W), row),
            pl.BlockSpec((TM, SGU_W), row),
            pl.BlockSpec((Q_W, d), const),
            pl.BlockSpec((SGU_W, d), const),
            pl.BlockSpec((d, d), const),
        ],
        out_specs=pl.BlockSpec((TM, d), row),
        out_shape=jax.ShapeDtypeStruct((t, d), F32),
        compiler_params=_params("parallel"),
        name="merge",
    )(x2, g, wg, oc, osel, ow, yb, pa, pb, wo)


def _rot_cols(w, heads):
    d = w.shape[0]
    w4 = w.reshape(d, heads, 2, HEAD_DIM // 2)
    return jnp.concatenate([-w4[:, :, 1], w4[:, :, 0]], axis=-1).reshape(d, heads * HEAD_DIM)


def _rope_tables(pos):
    inv = ROPE_THETA ** (-jnp.arange(0, HEAD_DIM, 2, dtype=F32) / HEAD_DIM)
    ang = pos.astype(F32)[:, None] * inv[None, :]
    cos = jnp.concatenate([jnp.cos(ang)] * (2 * LANES // HEAD_DIM), axis=-1)
    sin = jnp.concatenate([jnp.sin(ang)] * (2 * LANES // HEAD_DIM), axis=-1)
    return cos, sin


def _inproj_weight(w_in):
    q = w_in[:, 0:Q_W]
    ks = w_in[:, _OFF_KS:_OFF_KS + KV_W]
    kw = w_in[:, _OFF_KW:_OFF_KW + KV_W]
    gn0 = Q_W + 6 * KV_W
    gcols = np.zeros((N_KV_GROUPS, LANES), np.int32)
    gmask = np.zeros((N_KV_GROUPS, LANES), np.float32)
    for g in range(N_KV_GROUPS):
        for c in range(3):
            for hh in range(HEADS_PER_GROUP):
                gcols[g, c * HEADS_PER_GROUP + hh] = gn0 + (g * HEADS_PER_GROUP + hh) * 3 + c
                gmask[g, c * HEADS_PER_GROUP + hh] = 1.0
    gn = w_in[:, gcols.reshape(-1)] * gmask.reshape(1, -1)
    uv0 = gn0 + NSA_GATE_W
    uv = w_in[:, uv0:uv0 + 2 * SGU_W]
    w = jnp.concatenate([w_in[:, :gn0], _rot_cols(q, N_HEADS), _rot_cols(ks, N_KV_GROUPS),
                         _rot_cols(kw, N_KV_GROUPS), gn, uv], axis=1)
    assert w.shape[1] == _IN_COLS
    return w.astype(BF16), w_in[:, uv0 + 2 * SGU_W:].astype(BF16)


def _compress_params(pe_k, pe_v, w1k, w1v, w2k, w2v):
    half = L_CMP // 2

    def pe_rows(pe):
        rep = jnp.broadcast_to(pe[:, None, :], (L_CMP, N_KV_GROUPS, HEAD_DIM))
        return (rep[:half].reshape(1, -1), rep[half:].reshape(1, -1))

    def w1_halves(w1):
        w = w1.reshape(L_CMP, HEAD_DIM, PHI_HIDDEN)
        outs = []
        for part in (w[:half], w[half:]):
            z = jnp.zeros((half, N_KV_GROUPS, HEAD_DIM, N_KV_GROUPS, PHI_HIDDEN), F32)
            for g in range(N_KV_GROUPS):
                z = z.at[:, g, :, g, :].set(part)
            outs.append(z.reshape(half * N_KV_GROUPS * HEAD_DIM, N_KV_GROUPS * PHI_HIDDEN).astype(BF16))
        return outs

    zero = jnp.zeros_like(w2k)
    w2k_rot = _rot_cols(w2k, 1)
    w2k_all = jnp.concatenate([w2k, zero, zero, w2k, w2k_rot, zero, zero, w2k_rot], axis=1).astype(BF16)
    w2v_dup = jnp.concatenate([w2v, w2v], axis=1).astype(BF16)
    pkt, pkb = pe_rows(pe_k)
    pvt, pvb = pe_rows(pe_v)
    return (pkt, pkb, pvt, pvb), (*w1_halves(w1k), *w1_halves(w1v)), w2k_all, w2v_dup


def _sel_map_t(seq):
    n_rows = seq // CMP_STRIDE
    n_cmp = (seq - L_CMP) // CMP_STRIDE + 1
    cmp_start = np.arange(n_rows) * CMP_STRIDE
    sel_start = np.arange(LANES) * L_SEL
    ov = (np.minimum(cmp_start[None, :] + L_CMP, sel_start[:, None] + L_SEL)
          - np.maximum(cmp_start[None, :], sel_start[:, None]))
    m = np.clip(ov, 0, None).astype(np.float32) / L_CMP
    m[:, n_cmp:] = 0.0
    m[seq // L_SEL:, :] = 0.0
    return jnp.asarray(m, dtype=BF16)


def kernel(x, ffn1_norm, ffn1_w_gate_up, ffn1_w_down, mix_norm, w_in, cmp_pos_k, cmp_pos_v,
           phi_k_w1, phi_k_w2, phi_v_w1, phi_v_w2, sgu_norm, sgu_w_s, sgu_b_s, proj_a, proj_b,
           w_out, ffn2_norm, ffn2_w_gate_up, ffn2_w_down, final_norm):
    batch, seq, d = x.shape
    depth = w_in.shape[0]
    assert seq % TM == 0 and seq % TK == 0 and seq >= N_TOPK * L_SEL and seq // L_SEL <= LANES
    assert seq >= _WIN_KEYS
    t = batch * seq
    n_rows = seq // CMP_STRIDE

    cos, sin = _rope_tables(jnp.arange(seq))
    cos_c, sin_c = _rope_tables(jnp.arange(n_rows) * CMP_STRIDE + (L_CMP - 1))
    selt = _sel_map_t(seq)
    row = lambda v: v.reshape(1, -1)

    x2 = x.reshape(t, d)
    for l in range(depth):
        x2 = _ffn(x2, row(ffn1_norm[l]), ffn1_w_gate_up[l].astype(BF16), ffn1_w_down[l].astype(BF16),
                  row(final_norm), False)

        w_main, w_gates = _inproj_weight(w_in[l])
        bs = jnp.repeat(sgu_b_s[l].T, SGU_HEAD, axis=1)
        (q, kc, vc, ksa, ksb, vs2, kwa, kwb, vw2, gn, yb) = _inproj(
            x2, row(mix_norm[l]), w_main, cos, sin, row(sgu_norm[l]), sgu_w_s[l], bs, seq)

        pe, w1, w2k_all, w2v_dup = _compress_params(
            cmp_pos_k[l], cmp_pos_v[l], phi_k_w1[l], phi_v_w1[l], phi_k_w2[l], phi_v_w2[l])
        width = CMP_STRIDE * KV_W
        kca, kcb, vc2 = _compress(kc.reshape(t // CMP_STRIDE, width), vc.reshape(t // CMP_STRIDE, width),
                                  pe, w1, w2k_all, w2v_dup, cos_c, sin_c, batch)

        o_cmp, bias = _cmp_attention(q, kca, kcb, vc2, gn, selt, batch, seq)
        o_sel = _sel_attention(q, bias, ksa, ksb, vs2, gn, batch, seq)
        o_win = _win_attention(q, kwa, kwb, vw2, gn, batch, seq)

        x2 = _merge(x2, row(mix_norm[l]), w_gates, o_cmp, o_sel, o_win, yb,
                    proj_a[l].astype(BF16), proj_b[l].astype(BF16), w_out[l].astype(BF16))

        x2 = _ffn(x2, row(ffn2_norm[l]), ffn2_w_gate_up[l].astype(BF16), ffn2_w_down[l].astype(BF16),
                  row(final_norm), l == depth - 1)
    return x2.reshape(batch, seq, d)
```

```python
import functools

import jax
import jax.numpy as jnp
import numpy as np
from jax import lax
from jax.experimental import pallas as pl
from jax.experimental.pallas import tpu as pltpu

HEAD_DIM = 64
N_HEADS = 8
N_KV_GROUPS = 2
HEADS_PER_GROUP = N_HEADS // N_KV_GROUPS
HEAD_PAIRS = HEADS_PER_GROUP // 2
Q_W = N_HEADS * HEAD_DIM
KV_W = N_KV_GROUPS * HEAD_DIM
NSA_GATE_W = N_HEADS * 3
L_CMP = 32
CMP_STRIDE = 16
L_SEL = 64
N_TOPK = 16
W_WIN = 512
PHI_HIDDEN = 256
SGU_GROUPS = 8
SGU_HEAD = 64
SGU_W = SGU_GROUPS * SGU_HEAD
CHUNK = 128
ROPE_THETA = 10000.0
EPS = 1e-6
NEG_INF = -1e30
FORCE = 1e9
SCALE = HEAD_DIM ** -0.5
LOG2E = float(np.log2(np.e))
V_ROWS = HEAD_DIM + 16

LANES = 128
HALF = LANES // 2
VMEM_LIMIT = 56 * 1024 * 1024

TM = 512
TQ = 128
TQC = 256
TK = 512
FF_CHUNK = 256

F32 = jnp.float32
BF16 = jnp.bfloat16


def _dot(a, b):
    return jnp.dot(a, b, preferred_element_type=F32)


def _dot_nt(a, b):
    return lax.dot_general(a, b, (((1,), (1,)), ((), ())), preferred_element_type=F32)


def _rms(x, g):
    return x * lax.rsqrt(jnp.mean(x * x, axis=-1, keepdims=True) + EPS) * g


def _gelu_tanh(x):
    c = np.float32(np.sqrt(2.0 / np.pi))
    return 0.5 * x * (1.0 + jnp.tanh(c * (x + 0.044715 * (x * x * x))))


def _params(*sem):
    return pltpu.CompilerParams(dimension_semantics=sem, vmem_limit_bytes=VMEM_LIMIT)


def _ffn_body(x_ref, g_ref, wgu_ref, wd_ref, fg_ref, o_ref, *, d_ff, final_norm):
    x = x_ref[...]
    h = _rms(x, g_ref[...]).astype(BF16)
    acc = jnp.zeros(x.shape, F32)
    for c in range(d_ff // FF_CHUNK):
        lo = c * FF_CHUNK
        gate = _dot(h, wgu_ref[:, lo:lo + FF_CHUNK])
        up = _dot(h, wgu_ref[:, d_ff + lo:d_ff + lo + FF_CHUNK])
        act = (gate * jax.nn.sigmoid(gate) * up).astype(BF16)
        acc = acc + _dot(act, wd_ref[lo:lo + FF_CHUNK, :])
    y = x + 0.5 * acc
    if final_norm:
        y = _rms(y, fg_ref[...])
    o_ref[...] = y


def _ffn(x2, g, wgu, wd, fg, final_norm):
    t, d = x2.shape
    d_ff = wd.shape[0]
    assert t % TM == 0 and d_ff % FF_CHUNK == 0
    const = lambda i: (0, 0)
    return pl.pallas_call(
        functools.partial(_ffn_body, d_ff=d_ff, final_norm=final_norm),
        grid=(t // TM,),
        in_specs=[
            pl.BlockSpec((TM, d), lambda i: (i, 0)),
            pl.BlockSpec((1, d), const),
            pl.BlockSpec((d, 2 * d_ff), const, pipeline_mode=pl.Buffered(1)),
            pl.BlockSpec((d_ff, d), const, pipeline_mode=pl.Buffered(1)),
            pl.BlockSpec((1, d), const),
        ],
        out_specs=pl.BlockSpec((TM, d), lambda i: (i, 0)),
        out_shape=jax.ShapeDtypeStruct((t, d), F32),
        compiler_params=_params("parallel"),
        name="ffn",
    )(x2, g, wgu, wd, fg)


_OFF_Q, _OFF_KC, _OFF_VC, _OFF_KS, _OFF_VS, _OFF_KW, _OFF_VW = 0, 512, 640, 768, 896, 1024, 1152
_OFF_RQ, _OFF_RKS, _OFF_RKW, _OFF_GN, _OFF_UV = 1280, 1792, 1920, 2048, 2304
_IN_COLS = _OFF_UV + 2 * SGU_W


def _inproj_body(x_ref, g_ref, w_ref, cos_ref, sin_ref, sgn_ref, ws_ref, bs_ref,
                 q_ref, kc_ref, vc_ref, ksa_ref, ksb_ref, vst_ref, kwa_ref, kwb_ref, vwt_ref,
                 gn_ref, yb_ref, *, tiles_per_seq):
    h = _rms(x_ref[...], g_ref[...]).astype(BF16)
    p = _dot(h, w_ref[...])
    cos = cos_ref[...]
    sin = sin_ref[...]
    cos4 = jnp.concatenate([cos] * (Q_W // LANES), axis=1)
    sin4 = jnp.concatenate([sin] * (Q_W // LANES), axis=1)

    q = p[:, _OFF_Q:_OFF_Q + Q_W] * cos4 + p[:, _OFF_RQ:_OFF_RQ + Q_W] * sin4
    q_ref[...] = (q * (SCALE * LOG2E)).astype(BF16)
    kc_ref[...] = p[:, _OFF_KC:_OFF_KC + KV_W]
    vc_ref[...] = p[:, _OFF_VC:_OFF_VC + KV_W]

    ksel = p[:, _OFF_KS:_OFF_KS + KV_W] * cos + p[:, _OFF_RKS:_OFF_RKS + KV_W] * sin
    kwin = p[:, _OFF_KW:_OFF_KW + KV_W] * cos + p[:, _OFF_RKW:_OFF_RKW + KV_W] * sin

    shape = ksel.shape
    lane = lax.broadcasted_iota(jnp.int32, shape, 1)
    row = lax.broadcasted_iota(jnp.int32, shape, 0)
    lo = lane < HALF
    pos = (pl.program_id(0) % tiles_per_seq) * TM + row
    onehot = jnp.where(lane == jnp.right_shift(pos, 6), 1.0, 0.0).astype(BF16)
    ksel_sw = pltpu.roll(ksel, HALF, 1)
    kwin_sw = pltpu.roll(kwin, HALF, 1)
    for g, (ks_lo, ks_hi, kw_lo, kw_hi) in enumerate(((ksel, ksel_sw, kwin, kwin_sw),
                                                      (ksel_sw, ksel, kwin_sw, kwin))):
        ksa_ref[g] = jnp.concatenate([jnp.where(lo, ks_lo, 0.0).astype(BF16), onehot], axis=1)
        ksb_ref[g] = jnp.concatenate([jnp.where(lo, 0.0, ks_hi).astype(BF16), onehot], axis=1)
        kwa_ref[g] = jnp.where(lo, kw_lo, 0.0).astype(BF16)
        kwb_ref[g] = jnp.where(lo, 0.0, kw_hi).astype(BF16)

    vsel_t = p[:, _OFF_VS:_OFF_VS + KV_W].T
    vwin_t = p[:, _OFF_VW:_OFF_VW + KV_W].T
    ones_row = jnp.where(lax.broadcasted_iota(jnp.int32, (V_ROWS - HEAD_DIM, TM), 0) == 0, 1.0, 0.0)
    for g in range(N_KV_GROUPS):
        rows = slice(g * HEAD_DIM, (g + 1) * HEAD_DIM)
        vst_ref[g, 0] = jnp.concatenate([vsel_t[rows, :], ones_row], axis=0).astype(BF16)
        vwt_ref[g, 0] = jnp.concatenate([vwin_t[rows, :], ones_row], axis=0).astype(BF16)

    gn_ref[...] = jax.nn.sigmoid(p[:, _OFF_GN:_OFF_GN + 2 * LANES])

    ge = _gelu_tanh(p[:, _OFF_UV:_OFF_UV + 2 * SGU_W])
    u = ge[:, :SGU_W]
    v = _rms(ge[:, SGU_W:], sgn_ref[...])
    tri = (lax.broadcasted_iota(jnp.int32, (CHUNK, CHUNK), 0)
           >= lax.broadcasted_iota(jnp.int32, (CHUNK, CHUNK), 1))
    lo_c = lax.broadcasted_iota(jnp.int32, (CHUNK, LANES), 1) < HALF
    n_chunks = TM // CHUNK
    cols = []
    for pr in range(SGU_W // LANES):
        w_even = jnp.where(tri, ws_ref[2 * pr], 0.0).astype(BF16)
        w_odd = jnp.where(tri, ws_ref[2 * pr + 1], 0.0).astype(BF16)
        rows = []
        for r in range(n_chunks):
            vp = v[r * CHUNK:(r + 1) * CHUNK, pr * LANES:(pr + 1) * LANES]
            v_even = jnp.where(lo_c, vp, 0.0).astype(BF16)
            v_odd = jnp.where(lo_c, 0.0, vp).astype(BF16)
            rows.append(_dot(w_even, v_even) + _dot(w_odd, v_odd))
        cols.append(jnp.concatenate(rows, axis=0))
    vo = jnp.concatenate(cols, axis=1)
    bias = jnp.concatenate([bs_ref[...]] * n_chunks, axis=0)
    yb_ref[...] = (u * (vo + bias)).astype(BF16)


def _inproj(x2, g, w, cos, sin, sgn, ws, bs, seq):
    t, d = x2.shape
    assert TM == TK
    tiles_per_seq = seq // TM
    const2 = lambda i: (0, 0)
    row = lambda i: (i, 0)
    grp = lambda i: (0, i, 0)
    grp4 = lambda i: (0, i, 0, 0)
    pos = lambda i: (i % tiles_per_seq, 0)
    ng = N_KV_GROUPS
    return pl.pallas_call(
        functools.partial(_inproj_body, tiles_per_seq=tiles_per_seq),
        grid=(t // TM,),
        in_specs=[
            pl.BlockSpec((TM, d), row),
            pl.BlockSpec((1, d), const2),
            pl.BlockSpec((d, _IN_COLS), const2, pipeline_mode=pl.Buffered(1)),
            pl.BlockSpec((TM, LANES), pos),
            pl.BlockSpec((TM, LANES), pos),
            pl.BlockSpec((1, SGU_W), const2),
            pl.BlockSpec((SGU_GROUPS, CHUNK, CHUNK), lambda i: (0, 0, 0)),
            pl.BlockSpec((CHUNK, SGU_W), const2),
        ],
        out_specs=[
            pl.BlockSpec((TM, Q_W), row),
            pl.BlockSpec((TM, KV_W), row),
            pl.BlockSpec((TM, KV_W), row),
            pl.BlockSpec((ng, TM, 2 * LANES), grp),
            pl.BlockSpec((ng, TM, 2 * LANES), grp),
            pl.BlockSpec((ng, 1, V_ROWS, TK), grp4),
            pl.BlockSpec((ng, TM, LANES), grp),
            pl.BlockSpec((ng, TM, LANES), grp),
            pl.BlockSpec((ng, 1, V_ROWS, TK), grp4),
            pl.BlockSpec((TM, 2 * LANES), row),
            pl.BlockSpec((TM, SGU_W), row),
        ],
        out_shape=[
            jax.ShapeDtypeStruct((t, Q_W), BF16),
            jax.ShapeDtypeStruct((t, KV_W), F32),
            jax.ShapeDtypeStruct((t, KV_W), F32),
            jax.ShapeDtypeStruct((ng, t, 2 * LANES), BF16),
            jax.ShapeDtypeStruct((ng, t, 2 * LANES), BF16),
            jax.ShapeDtypeStruct((ng, t // TK, V_ROWS, TK), BF16),
            jax.ShapeDtypeStruct((ng, t, LANES), BF16),
            jax.ShapeDtypeStruct((ng, t, LANES), BF16),
            jax.ShapeDtypeStruct((ng, t // TK, V_ROWS, TK), BF16),
            jax.ShapeDtypeStruct((t, 2 * LANES), F32),
            jax.ShapeDtypeStruct((t, SGU_W), BF16),
        ],
        compiler_params=_params("parallel"),
        name="inproj",
    )(x2, g, w, cos, sin, sgn, ws, bs)


def _compress_body(kc_ref, vc_ref, pekt_ref, pekb_ref, pevt_ref, pevb_ref,
                   wkt_ref, wkb_ref, wvt_ref, wvb_ref, w2k_ref, w2vt_ref, cos_ref, sin_ref,
                   kca_ref, kcb_ref, vct_ref):
    n_rows = kc_ref.shape[0]

    def hidden(c_ref, pet_ref, peb_ref, wt_ref, wb_ref):
        c = c_ref[...]
        top = _dot((c + pet_ref[...]).astype(BF16), wt_ref[...])
        bot = _dot((c + peb_ref[...]).astype(BF16), wb_ref[...])
        return _gelu_tanh(top + pltpu.roll(bot, n_rows - 1, 0))

    hk = hidden(kc_ref, pekt_ref, pekb_ref, wkt_ref, wkb_ref)
    hv = hidden(vc_ref, pevt_ref, pevb_ref, wvt_ref, wvb_ref)
    cos = cos_ref[...]
    sin = sin_ref[...]
    for g in range(N_KV_GROUPS):
        hkg = hk[:, g * PHI_HIDDEN:(g + 1) * PHI_HIDDEN].astype(BF16)
        hvg = hv[:, g * PHI_HIDDEN:(g + 1) * PHI_HIDDEN].astype(BF16)
        kk = _dot(hkg, w2k_ref[...])
        kca_ref[g] = (kk[:, 0:LANES] * cos + kk[:, 2 * LANES:3 * LANES] * sin).astype(BF16)
        kcb_ref[g] = (kk[:, LANES:2 * LANES] * cos + kk[:, 3 * LANES:4 * LANES] * sin).astype(BF16)
        vct_ref[g, 0] = _dot_nt(w2vt_ref[...], hvg).astype(BF16)


def _compress(kc_rows, vc_rows, pe, w1, w2k, w2vt, cos_c, sin_c, batch):
    total, width = kc_rows.shape
    n_rows = total // batch
    const = lambda b: (0, 0)
    k_shape = jax.ShapeDtypeStruct((N_KV_GROUPS, total, LANES), BF16)
    k_spec = pl.BlockSpec((N_KV_GROUPS, n_rows, LANES), lambda b: (0, b, 0))
    pe_spec = pl.BlockSpec((1, width), const)
    w1_spec = pl.BlockSpec((width, N_KV_GROUPS * PHI_HIDDEN), const)
    return pl.pallas_call(
        _compress_body,
        grid=(batch,),
        in_specs=[
            pl.BlockSpec((n_rows, width), lambda b: (b, 0)),
            pl.BlockSpec((n_rows, width), lambda b: (b, 0)),
            pe_spec, pe_spec, pe_spec, pe_spec,
            w1_spec, w1_spec, w1_spec, w1_spec,
            pl.BlockSpec((PHI_HIDDEN, 4 * LANES), const),
            pl.BlockSpec((HEAD_DIM, PHI_HIDDEN), const),
            pl.BlockSpec((n_rows, LANES), const),
            pl.BlockSpec((n_rows, LANES), const),
        ],
        out_specs=[k_spec, k_spec,
                   pl.BlockSpec((N_KV_GROUPS, 1, HEAD_DIM, n_rows), lambda b: (0, b, 0, 0))],
        out_shape=[k_shape, k_shape,
                   jax.ShapeDtypeStruct((N_KV_GROUPS, batch, HEAD_DIM, n_rows), BF16)],
        compiler_params=_params("parallel"),
        name="compress",
    )(kc_rows, vc_rows, *pe, *w1, w2k, w2vt, cos_c, sin_c)


def _cmp_body(q_ref, kca_ref, kcb_ref, vct_ref, gn_ref, selt_ref, o_ref, bias_ref):
    n_cmp = kca_ref.shape[1]
    n_col = HEADS_PER_GROUP * TQC
    t0 = pl.program_id(2) * TQC
    lq = jnp.concatenate([q_ref[:, pr * LANES:(pr + 1) * LANES] for pr in range(HEAD_PAIRS)], axis=0)
    s = jnp.concatenate([_dot_nt(kca_ref[0], lq), _dot_nt(kcb_ref[0], lq)], axis=1)
    tok = t0 + jnp.bitwise_and(lax.broadcasted_iota(jnp.int32, (1, n_col), 1), TQC - 1)
    cmp_end = lax.broadcasted_iota(jnp.int32, (n_cmp, 1), 0) * CMP_STRIDE + (L_CMP - 1)
    s = jnp.where(cmp_end <= tok, s, NEG_INF)
    e = jnp.exp2(s - jnp.max(s, axis=0, keepdims=True))
    any_valid = jnp.where(tok >= L_CMP - 1, 1.0, 0.0)
    p = e * (any_valid / jnp.sum(e, axis=0, keepdims=True))
    oc_t = _dot(vct_ref[0, 0], p.astype(BF16))
    psum = p[:, 0:TQC]
    for hh in range(1, HEADS_PER_GROUP):
        psum = psum + p[:, hh * TQC:(hh + 1) * TQC]

    selt = selt_ref[...]
    p_hi = psum.astype(BF16)
    r1 = psum - p_hi.astype(F32)
    p_mid = r1.astype(BF16)
    p_lo = (r1 - p_mid.astype(F32)).astype(BF16)
    imp = _dot(selt, p_hi) + _dot(selt, p_mid) + _dot(selt, p_lo)

    n_blk = imp.shape[0]
    jj = lax.broadcasted_iota(jnp.int32, (n_blk, TQC), 0)
    cur = jnp.right_shift(t0 + lax.broadcasted_iota(jnp.int32, (n_blk, TQC), 1), 6)
    jf = jj.astype(F32)
    val = jnp.where(jj == cur, FORCE, jnp.where(jj > cur, -FORCE, imp))
    sel = jnp.zeros((n_blk, TQC), F32)
    for _ in range(N_TOPK):
        best = jnp.max(val, axis=0, keepdims=True)
        first = jnp.min(jnp.where(val == best, jf, 1e6), axis=0, keepdims=True)
        hit = jf == first
        sel = jnp.where(hit, 1.0, sel)
        val = jnp.where(hit, -3e38, val)
    bias_t = jnp.where(jnp.logical_and(sel > 0.5, jj <= cur), 0.0, NEG_INF)

    for u in range(TQC // LANES):
        rows = slice(u * LANES, (u + 1) * LANES)
        gn_t = gn_ref[rows, :].T
        bias_ref[0, rows, :] = bias_t[:, rows].T.astype(BF16)
        for pr in range(HEAD_PAIRS):
            c_even = pr * TQC + u * LANES
            c_odd = (HEAD_PAIRS + pr) * TQC + u * LANES
            even = oc_t[:, c_even:c_even + LANES] * gn_t[2 * pr:2 * pr + 1, :]
            odd = oc_t[:, c_odd:c_odd + LANES] * gn_t[2 * pr + 1:2 * pr + 2, :]
            o_ref[rows, pr * LANES:(pr + 1) * LANES] = jnp.concatenate([even, odd], axis=0).T


def _cmp_attention(q, kca, kcb, vct, gn, selt, batch, seq):
    t = q.shape[0]
    n_cmp = kca.shape[1] // batch
    nq = seq // TQC
    k_spec = pl.BlockSpec((1, n_cmp, LANES), lambda b, g, i: (g, b, 0))
    return pl.pallas_call(
        _cmp_body,
        grid=(batch, N_KV_GROUPS, nq),
        in_specs=[
            pl.BlockSpec((TQC, 2 * LANES), lambda b, g, i: (b * nq + i, g)),
            k_spec, k_spec,
            pl.BlockSpec((1, 1, HEAD_DIM, n_cmp), lambda b, g, i: (g, b, 0, 0)),
            pl.BlockSpec((TQC, LANES), lambda b, g, i: (b * nq + i, g)),
            pl.BlockSpec(selt.shape, lambda b, g, i: (0, 0)),
        ],
        out_specs=[
            pl.BlockSpec((TQC, 2 * LANES), lambda b, g, i: (b * nq + i, g)),
            pl.BlockSpec((1, TQC, LANES), lambda b, g, i: (g, b * nq + i, 0)),
        ],
        out_shape=[
            jax.ShapeDtypeStruct((t, Q_W), F32),
            jax.ShapeDtypeStruct((N_KV_GROUPS, t, LANES), BF16),
        ],
        compiler_params=_params("parallel", "parallel", "parallel"),
        name="cmp_attn",
    )(q, kca, kcb, vct, gn, selt)


def _selwin_body(q_ref, bias_ref, ksa_ref, ksb_ref, vst_ref, kwa_ref, kwb_ref, vwt_ref, gn_ref, oc_ref,
                 o_ref, acc_ref, sa_ref, sb_ref, pa_ref, pb_ref):
    qi = pl.program_id(2)
    t0 = qi * TQ
    n_col = HEADS_PER_GROUP * TQ
    q_pairs = [q_ref[:, pr * LANES:(pr + 1) * LANES] for pr in range(HEAD_PAIRS)]
    lq = jnp.concatenate(q_pairs, axis=0)
    bias = bias_ref[0]
    ls = jnp.concatenate([jnp.concatenate([qp, bias], axis=1) for qp in q_pairs], axis=0)
    tok = t0 + jnp.bitwise_and(lax.broadcasted_iota(jnp.int32, (1, n_col), 1), TQ - 1)
    key_row = lax.broadcasted_iota(jnp.int32, (TK, 1), 0)
    jd = t0 // TK

    def scores(k_a, k_b, lhs, j, mask):
        k_lo = pl.multiple_of(j * TK, TK)
        s = jnp.concatenate([_dot_nt(k_a[0, pl.ds(k_lo, TK), :], lhs),
                             _dot_nt(k_b[0, pl.ds(k_lo, TK), :], lhs)], axis=1)
        if mask is not None:
            s = jnp.where(mask, s, NEG_INF)
        return s, jnp.max(s, axis=0, keepdims=True)

    def weights(s, mx, m_old):
        m_new = jnp.maximum(m_old, mx)
        return m_new, jnp.exp2(m_old - m_new), jnp.exp2(s - m_new).astype(BF16)

    def causal(j):
        return j * TK + key_row <= tok

    def window(j):
        key = j * TK + key_row
        dist = tok - key
        return jnp.logical_and(jnp.logical_and(dist >= 0, dist < W_WIN), key >= 0)

    sel_keys = (ksa_ref, ksb_ref, ls)
    win_keys = (kwa_ref, kwb_ref, lq)

    neg_inf = jnp.full((1, n_col), -jnp.inf, F32)
    acc_ref[...] = jnp.zeros(acc_ref.shape, F32)
    pb_ref[...] = jnp.zeros(pb_ref.shape, BF16)
    sa_ref[...], mx0 = scores(*sel_keys, 0, None)
    sb_ref[...], mx1 = scores(*sel_keys, 1, None)

    def pair(pi, carry):
        m, mx_a, mx_b, a_prev = carry
        j = 2 * pi
        m, a_cur, pa_ref[...] = weights(sa_ref[...], mx_a, m)
        sa_ref[...], mx_a = scores(*sel_keys, j + 2, None)
        acc_ref[...] = a_prev * acc_ref[...] + _dot(vst_ref[0, jnp.maximum(j - 1, 0)], pb_ref[...])
        m, a_next, pb_ref[...] = weights(sb_ref[...], mx_b, m)
        sb_ref[...], mx_b = scores(*sel_keys, j + 3, None)
        acc_ref[...] = a_cur * acc_ref[...] + _dot(vst_ref[0, j], pa_ref[...])
        return m, mx_a, mx_b, a_next

    m, _, _, a_prev = lax.fori_loop(0, jd // 2, pair, (neg_inf, mx0, mx1, jnp.ones((1, n_col), F32)))

    jf = 2 * (jd // 2)
    s_a = jnp.where(causal(jf), sa_ref[...], NEG_INF)
    m, a_cur, pa_ref[...] = weights(s_a, jnp.max(s_a, axis=0, keepdims=True), m)
    sa_ref[...], mx_w0 = scores(*win_keys, jnp.maximum(jd - 1, 0), window(jd - 1))
    acc_ref[...] = a_prev * acc_ref[...] + _dot(vst_ref[0, jnp.maximum(jf - 1, 0)], pb_ref[...])
    s_b = jnp.where(causal(jf + 1), sb_ref[...], NEG_INF)
    _, a_next, pb_ref[...] = weights(s_b, jnp.max(s_b, axis=0, keepdims=True), m)
    sb_ref[...], mx_w1 = scores(*win_keys, jd, window(jd))
    acc_ref[...] = a_cur * acc_ref[...] + _dot(vst_ref[0, jf], pa_ref[...])
    m_w, _, pa_ref[...] = weights(sa_ref[...], mx_w0, neg_inf)
    o_sel = a_next * acc_ref[...] + _dot(vst_ref[0, jf + 1], pb_ref[...])
    _, a_win, pb_ref[...] = weights(sb_ref[...], mx_w1, m_w)
    o_win = a_win * _dot(vwt_ref[0, jnp.maximum(jd - 1, 0)], pa_ref[...]) + _dot(vwt_ref[0, jd], pb_ref[...])
    os_t = o_sel[:HEAD_DIM] * (1.0 / o_sel[HEAD_DIM:HEAD_DIM + 1])
    ow_t = o_win[:HEAD_DIM] * (1.0 / o_win[HEAD_DIM:HEAD_DIM + 1])

    gn_t = gn_ref[...].T
    for pr in range(HEAD_PAIRS):
        halves = []
        for e in range(2):
            c0 = (e * HEAD_PAIRS + pr) * TQ
            hh = 2 * pr + e
            g_sel = gn_t[HEADS_PER_GROUP + hh:HEADS_PER_GROUP + hh + 1, :]
            g_win = gn_t[2 * HEADS_PER_GROUP + hh:2 * HEADS_PER_GROUP + hh + 1, :]
            halves.append(os_t[:, c0:c0 + TQ] * g_sel + ow_t[:, c0:c0 + TQ] * g_win)
        cols = slice(pr * LANES, (pr + 1) * LANES)
        o_ref[:, cols] = (jnp.concatenate(halves, axis=0).T + oc_ref[:, cols]).astype(BF16)


def _selwin_attention(q, bias, ksa, ksb, vst, kwa, kwb, vwt, gn, oc, batch, seq):
    t = q.shape[0]
    nq = seq // TQ
    n_col = HEADS_PER_GROUP * TQ
    tile_row = lambda b, g, i: (b * nq + i, g)
    grp = lambda b, g, i: (g, b, 0)
    grp4 = lambda b, g, i: (g, b, 0, 0)
    return pl.pallas_call(
        _selwin_body,
        grid=(batch, N_KV_GROUPS, nq),
        in_specs=[
            pl.BlockSpec((TQ, 2 * LANES), tile_row),
            pl.BlockSpec((1, TQ, LANES), lambda b, g, i: (g, b * nq + i, 0)),
            pl.BlockSpec((1, seq, 2 * LANES), grp),
            pl.BlockSpec((1, seq, 2 * LANES), grp),
            pl.BlockSpec((1, seq // TK, V_ROWS, TK), grp4),
            pl.BlockSpec((1, seq, LANES), grp),
            pl.BlockSpec((1, seq, LANES), grp),
            pl.BlockSpec((1, seq // TK, V_ROWS, TK), grp4),
            pl.BlockSpec((TQ, LANES), tile_row),
            pl.BlockSpec((TQ, 2 * LANES), tile_row),
        ],
        out_specs=pl.BlockSpec((TQ, 2 * LANES), tile_row),
        out_shape=jax.ShapeDtypeStruct((t, Q_W), BF16),
        scratch_shapes=[pltpu.VMEM((V_ROWS, n_col), F32),
                        pltpu.VMEM((TK, n_col), F32), pltpu.VMEM((TK, n_col), F32),
                        pltpu.VMEM((TK, n_col), BF16), pltpu.VMEM((TK, n_col), BF16)],
        compiler_params=_params("parallel", "parallel", "arbitrary"),
        name="selwin_attn",
    )(q, bias, ksa, ksb, vst, kwa, kwb, vwt, gn, oc)


def _merge_body(x_ref, g_ref, wg_ref, ya_ref, yb_ref, pa_ref, pb_ref, wo_ref, o_ref):
    x = x_ref[...]
    d = x.shape[1]
    h = _rms(x, g_ref[...]).astype(BF16)
    gates = jax.nn.sigmoid(_dot(h, wg_ref[...]))
    merged = gates[:, :d] * _dot(ya_ref[...], pa_ref[...]) + gates[:, d:] * _dot(yb_ref[...], pb_ref[...])
    o_ref[...] = x + _dot(merged.astype(BF16), wo_ref[...])


def _merge(x2, g, wg, ya, yb, pa, pb, wo):
    t, d = x2.shape
    const = lambda i: (0, 0)
    row = lambda i: (i, 0)
    return pl.pallas_call(
        _merge_body,
        grid=(t // TM,),
        in_specs=[
            pl.BlockSpec((TM, d), row),
            pl.BlockSpec((1, d), const),
            pl.BlockSpec((d, 2 * d), const),
            pl.BlockSpec((TM, Q_W), row),
            pl.BlockSpec((TM, SGU_W), row),
            pl.BlockSpec((Q_W, d), const),
            pl.BlockSpec((SGU_W, d), const),
            pl.BlockSpec((d, d), const),
        ],
        out_specs=pl.BlockSpec((TM, d), row),
        out_shape=jax.ShapeDtypeStruct((t, d), F32),
        compiler_params=_params("parallel"),
        name="merge",
    )(x2, g, wg, ya, yb, pa, pb, wo)


def _rot_cols(w, heads):
    d = w.shape[0]
    w4 = w.reshape(d, heads, 2, HEAD_DIM // 2)
    return jnp.concatenate([-w4[:, :, 1], w4[:, :, 0]], axis=-1).reshape(d, heads * HEAD_DIM)


def _rope_tables(pos):
    inv = ROPE_THETA ** (-jnp.arange(0, HEAD_DIM, 2, dtype=F32) / HEAD_DIM)
    ang = pos.astype(F32)[:, None] * inv[None, :]
    cos = jnp.concatenate([jnp.cos(ang)] * (2 * LANES // HEAD_DIM), axis=-1)
    sin = jnp.concatenate([jnp.sin(ang)] * (2 * LANES // HEAD_DIM), axis=-1)
    return cos, sin


def _inproj_weight(w_in):
    q = w_in[:, 0:Q_W]
    ks = w_in[:, _OFF_KS:_OFF_KS + KV_W]
    kw = w_in[:, _OFF_KW:_OFF_KW + KV_W]
    gn0 = Q_W + 6 * KV_W
    zero = jnp.zeros((w_in.shape[0], LANES - 3 * HEADS_PER_GROUP), w_in.dtype)
    gn = []
    for g in range(N_KV_GROUPS):
        w_g = w_in[:, gn0 + g * 3 * HEADS_PER_GROUP:gn0 + (g + 1) * 3 * HEADS_PER_GROUP]
        gn += [w_g.reshape(-1, HEADS_PER_GROUP, 3).transpose(0, 2, 1).reshape(-1, 3 * HEADS_PER_GROUP), zero]
    uv0 = gn0 + NSA_GATE_W
    uv = w_in[:, uv0:uv0 + 2 * SGU_W]
    w = jnp.concatenate([w_in[:, :gn0], _rot_cols(q, N_HEADS), _rot_cols(ks, N_KV_GROUPS),
                         _rot_cols(kw, N_KV_GROUPS), *gn, uv], axis=1)
    assert w.shape[1] == _IN_COLS
    return w.astype(BF16), w_in[:, uv0 + 2 * SGU_W:].astype(BF16)


def _compress_params(pe_k, pe_v, w1k, w1v, w2k, w2v):
    half = L_CMP // 2

    def pe_rows(pe):
        rep = jnp.broadcast_to(pe[:, None, :], (L_CMP, N_KV_GROUPS, HEAD_DIM))
        return (rep[:half].reshape(1, -1), rep[half:].reshape(1, -1))

    def w1_halves(w1):
        w = w1.reshape(L_CMP, HEAD_DIM, PHI_HIDDEN)
        zero = jnp.zeros((half, HEAD_DIM, PHI_HIDDEN), w.dtype)
        outs = []
        for part in (w[:half], w[half:]):
            blocks = jnp.stack([jnp.concatenate([part, zero], axis=-1),
                                jnp.concatenate([zero, part], axis=-1)], axis=1)
            outs.append(blocks.reshape(half * N_KV_GROUPS * HEAD_DIM, N_KV_GROUPS * PHI_HIDDEN).astype(BF16))
        return outs

    zero = jnp.zeros_like(w2k)
    w2k_rot = _rot_cols(w2k, 1)
    w2k_all = jnp.concatenate([w2k, zero, zero, w2k, w2k_rot, zero, zero, w2k_rot], axis=1).astype(BF16)
    pkt, pkb = pe_rows(pe_k)
    pvt, pvb = pe_rows(pe_v)
    return (pkt, pkb, pvt, pvb), (*w1_halves(w1k), *w1_halves(w1v)), w2k_all, w2v.T.astype(BF16)


def _sel_map_t(seq):
    n_rows = seq // CMP_STRIDE
    n_cmp = (seq - L_CMP) // CMP_STRIDE + 1
    cmp_start = np.arange(n_rows) * CMP_STRIDE
    sel_start = np.arange(LANES) * L_SEL
    ov = (np.minimum(cmp_start[None, :] + L_CMP, sel_start[:, None] + L_SEL)
          - np.maximum(cmp_start[None, :], sel_start[:, None]))
    m = np.clip(ov, 0, None).astype(np.float32) / L_CMP
    m[:, n_cmp:] = 0.0
    m[seq // L_SEL:, :] = 0.0
    return jnp.asarray(m, dtype=BF16)


def kernel(x, ffn1_norm, ffn1_w_gate_up, ffn1_w_down, mix_norm, w_in, cmp_pos_k, cmp_pos_v,
           phi_k_w1, phi_k_w2, phi_v_w1, phi_v_w2, sgu_norm, sgu_w_s, sgu_b_s, proj_a, proj_b,
           w_out, ffn2_norm, ffn2_w_gate_up, ffn2_w_down, final_norm):
    batch, seq, d = x.shape
    depth = w_in.shape[0]
    assert seq % TM == 0 and seq % TK == 0 and seq >= N_TOPK * L_SEL and seq // L_SEL <= LANES
    assert seq % (2 * TK) == 0 and W_WIN <= TK and N_KV_GROUPS == 2 and HEADS_PER_GROUP == 4
    t = batch * seq
    n_rows = seq // CMP_STRIDE

    cos, sin = _rope_tables(jnp.arange(seq))
    cos_c, sin_c = _rope_tables(jnp.arange(n_rows) * CMP_STRIDE + (L_CMP - 1))
    selt = _sel_map_t(seq)
    row = lambda v: v.reshape(1, -1)

    x2 = x.reshape(t, d)
    for l in range(depth):
        x2 = _ffn(x2, row(ffn1_norm[l]), ffn1_w_gate_up[l].astype(BF16), ffn1_w_down[l].astype(BF16),
                  row(final_norm), False)

        w_main, w_gates = _inproj_weight(w_in[l])
        bs = jnp.repeat(sgu_b_s[l].T, SGU_HEAD, axis=1)
        (q, kc, vc, ksa, ksb, vst, kwa, kwb, vwt, gn, yb) = _inproj(
            x2, row(mix_norm[l]), w_main, cos, sin, row(sgu_norm[l]), sgu_w_s[l], bs, seq)

        pe, w1, w2k_all, w2vt = _compress_params(
            cmp_pos_k[l], cmp_pos_v[l], phi_k_w1[l], phi_v_w1[l], phi_k_w2[l], phi_v_w2[l])
        width = CMP_STRIDE * KV_W
        kca, kcb, vct = _compress(kc.reshape(t // CMP_STRIDE, width), vc.reshape(t // CMP_STRIDE, width),
                                  pe, w1, w2k_all, w2vt, cos_c, sin_c, batch)

        o_cmp, bias = _cmp_attention(q, kca, kcb, vct, gn, selt, batch, seq)
        ya = _selwin_attention(q, bias, ksa, ksb, vst, kwa, kwb, vwt, gn, o_cmp, batch, seq)

        x2 = _merge(x2, row(mix_norm[l]), w_gates, ya, yb,
                    proj_a[l].astype(BF16), proj_b[l].astype(BF16), w_out[l].astype(BF16))

        x2 = _ffn(x2, row(ffn2_norm[l]), ffn2_w_gate_up[l].astype(BF16), ffn2_w_down[l].astype(BF16),
                  row(final_norm), l == depth - 1)
    return x2.reshape(batch, seq, d)
```

```python
import functools

import jax
import jax.numpy as jnp
import numpy as np
from jax import lax
from jax.experimental import pallas as pl
from jax.experimental.pallas import tpu as pltpu

HEAD_DIM = 64
N_HEADS = 8
N_KV_GROUPS = 2
HEADS_PER_GROUP = N_HEADS // N_KV_GROUPS
HEAD_PAIRS = HEADS_PER_GROUP // 2
Q_W = N_HEADS * HEAD_DIM
KV_W = N_KV_GROUPS * HEAD_DIM
NSA_GATE_W = N_HEADS * 3
L_CMP = 32
CMP_STRIDE = 16
L_SEL = 64
N_TOPK = 16
W_WIN = 512
PHI_HIDDEN = 256
SGU_GROUPS = 8
SGU_HEAD = 64
SGU_W = SGU_GROUPS * SGU_HEAD
CHUNK = 128
ROPE_THETA = 10000.0
EPS = 1e-6
NEG_INF = -1e30
FORCE = 1e9
SCALE = HEAD_DIM ** -0.5
LOG2E = float(np.log2(np.e))
V_ROWS = HEAD_DIM + 16

LANES = 128
HALF = LANES // 2
VMEM_LIMIT = 56 * 1024 * 1024

TM = 512
TQ = 128
TQC = 256
TK = 512
FF_CHUNK = 256

F32 = jnp.float32
BF16 = jnp.bfloat16


def _dot(a, b):
    return jnp.dot(a, b, preferred_element_type=F32)


def _dot_nt(a, b):
    return lax.dot_general(a, b, (((1,), (1,)), ((), ())), preferred_element_type=F32)


def _rms(x, g):
    return x * lax.rsqrt(jnp.mean(x * x, axis=-1, keepdims=True) + EPS) * g


def _gelu_tanh(x):
    c = np.float32(np.sqrt(2.0 / np.pi))
    return 0.5 * x * (1.0 + jnp.tanh(c * (x + 0.044715 * (x * x * x))))


def _params(*sem):
    return pltpu.CompilerParams(dimension_semantics=sem, vmem_limit_bytes=VMEM_LIMIT)


def _ffn_body(x_ref, g_ref, wgu_ref, wd_ref, fg_ref, o_ref, *, d_ff, final_norm):
    x = x_ref[...]
    h = _rms(x, g_ref[...]).astype(BF16)
    acc = jnp.zeros(x.shape, F32)
    for c in range(d_ff // FF_CHUNK):
        lo = c * FF_CHUNK
        gate = _dot(h, wgu_ref[:, lo:lo + FF_CHUNK])
        up = _dot(h, wgu_ref[:, d_ff + lo:d_ff + lo + FF_CHUNK])
        act = (gate * jax.nn.sigmoid(gate) * up).astype(BF16)
        acc = acc + _dot(act, wd_ref[lo:lo + FF_CHUNK, :])
    y = x + 0.5 * acc
    if final_norm:
        y = _rms(y, fg_ref[...])
    o_ref[...] = y


def _ffn(x2, g, wgu, wd, fg, final_norm):
    t, d = x2.shape
    d_ff = wd.shape[0]
    assert t % TM == 0 and d_ff % FF_CHUNK == 0
    const = lambda i: (0, 0)
    return pl.pallas_call(
        functools.partial(_ffn_body, d_ff=d_ff, final_norm=final_norm),
        grid=(t // TM,),
        in_specs=[
            pl.BlockSpec((TM, d), lambda i: (i, 0)),
            pl.BlockSpec((1, d), const),
            pl.BlockSpec((d, 2 * d_ff), const, pipeline_mode=pl.Buffered(1)),
            pl.BlockSpec((d_ff, d), const, pipeline_mode=pl.Buffered(1)),
            pl.BlockSpec((1, d), const),
        ],
        out_specs=pl.BlockSpec((TM, d), lambda i: (i, 0)),
        out_shape=jax.ShapeDtypeStruct((t, d), F32),
        compiler_params=_params("parallel"),
        name="ffn",
    )(x2, g, wgu, wd, fg)


_OFF_Q, _OFF_KC, _OFF_VC, _OFF_KS, _OFF_VS, _OFF_KW, _OFF_VW = 0, 512, 640, 768, 896, 1024, 1152
_OFF_RQ, _OFF_RKS, _OFF_RKW, _OFF_GN, _OFF_UV = 1280, 1792, 1920, 2048, 2304
_IN_COLS = _OFF_UV + 2 * SGU_W


def _inproj_body(x_ref, g_ref, w_ref, cos_ref, sin_ref, sgn_ref, ws_ref, bs_ref,
                 q_ref, kc_ref, vc_ref, ksa_ref, ksb_ref, vst_ref, vsd_ref, kwa_ref, kwb_ref, vwt_ref,
                 gn_ref, yb_ref, *, tiles_per_seq):
    h = _rms(x_ref[...], g_ref[...]).astype(BF16)
    p = _dot(h, w_ref[...])
    cos = cos_ref[...]
    sin = sin_ref[...]
    cos4 = jnp.concatenate([cos] * (Q_W // LANES), axis=1)
    sin4 = jnp.concatenate([sin] * (Q_W // LANES), axis=1)

    q = p[:, _OFF_Q:_OFF_Q + Q_W] * cos4 + p[:, _OFF_RQ:_OFF_RQ + Q_W] * sin4
    q_ref[...] = (q * (SCALE * LOG2E)).astype(BF16)
    kc_ref[...] = p[:, _OFF_KC:_OFF_KC + KV_W]
    vc_ref[...] = p[:, _OFF_VC:_OFF_VC + KV_W]

    ksel = p[:, _OFF_KS:_OFF_KS + KV_W] * cos + p[:, _OFF_RKS:_OFF_RKS + KV_W] * sin
    kwin = p[:, _OFF_KW:_OFF_KW + KV_W] * cos + p[:, _OFF_RKW:_OFF_RKW + KV_W] * sin

    shape = ksel.shape
    lane = lax.broadcasted_iota(jnp.int32, shape, 1)
    row = lax.broadcasted_iota(jnp.int32, shape, 0)
    lo = lane < HALF
    pos = (pl.program_id(0) % tiles_per_seq) * TM + row
    onehot = jnp.where(lane == jnp.right_shift(pos, 6), 1.0, 0.0).astype(BF16)
    ksel_sw = pltpu.roll(ksel, HALF, 1)
    kwin_sw = pltpu.roll(kwin, HALF, 1)
    for g, (ks_lo, ks_hi, kw_lo, kw_hi) in enumerate(((ksel, ksel_sw, kwin, kwin_sw),
                                                      (ksel_sw, ksel, kwin_sw, kwin))):
        ksa_ref[g] = jnp.concatenate([jnp.where(lo, ks_lo, 0.0).astype(BF16), onehot], axis=1)
        ksb_ref[g] = jnp.concatenate([jnp.where(lo, 0.0, ks_hi).astype(BF16), onehot], axis=1)
        kwa_ref[g] = jnp.where(lo, kw_lo, 0.0).astype(BF16)
        kwb_ref[g] = jnp.where(lo, 0.0, kw_hi).astype(BF16)

    vsel_t = p[:, _OFF_VS:_OFF_VS + KV_W].T
    vwin_t = p[:, _OFF_VW:_OFF_VW + KV_W].T
    ones_row = jnp.where(lax.broadcasted_iota(jnp.int32, (V_ROWS - HEAD_DIM, TM), 0) == 0, 1.0, 0.0)
    for g in range(N_KV_GROUPS):
        rows = slice(g * HEAD_DIM, (g + 1) * HEAD_DIM)
        vs_aug = jnp.concatenate([vsel_t[rows, :], ones_row], axis=0).astype(BF16)
        vw_aug = jnp.concatenate([vwin_t[rows, :], ones_row], axis=0).astype(BF16)
        vst_ref[g, 0] = vs_aug
        for r in range(TM // TQ):
            vsd_ref[g, r] = vs_aug[:, r * TQ:(r + 1) * TQ]
            vwt_ref[g, r] = vw_aug[:, r * TQ:(r + 1) * TQ]

    gn_ref[...] = jax.nn.sigmoid(p[:, _OFF_GN:_OFF_GN + 2 * LANES])

    ge = _gelu_tanh(p[:, _OFF_UV:_OFF_UV + 2 * SGU_W])
    u = ge[:, :SGU_W]
    v = _rms(ge[:, SGU_W:], sgn_ref[...])
    tri = (lax.broadcasted_iota(jnp.int32, (CHUNK, CHUNK), 0)
           >= lax.broadcasted_iota(jnp.int32, (CHUNK, CHUNK), 1))
    lo_c = lax.broadcasted_iota(jnp.int32, (CHUNK, LANES), 1) < HALF
    n_chunks = TM // CHUNK
    cols = []
    for pr in range(SGU_W // LANES):
        w_even = jnp.where(tri, ws_ref[2 * pr], 0.0).astype(BF16)
        w_odd = jnp.where(tri, ws_ref[2 * pr + 1], 0.0).astype(BF16)
        rows = []
        for r in range(n_chunks):
            vp = v[r * CHUNK:(r + 1) * CHUNK, pr * LANES:(pr + 1) * LANES]
            v_even = jnp.where(lo_c, vp, 0.0).astype(BF16)
            v_odd = jnp.where(lo_c, 0.0, vp).astype(BF16)
            rows.append(_dot(w_even, v_even) + _dot(w_odd, v_odd))
        cols.append(jnp.concatenate(rows, axis=0))
    vo = jnp.concatenate(cols, axis=1)
    bias = jnp.concatenate([bs_ref[...]] * n_chunks, axis=0)
    yb_ref[...] = (u * (vo + bias)).astype(BF16)


def _inproj(x2, g, w, cos, sin, sgn, ws, bs, seq):
    t, d = x2.shape
    assert TM == TK
    tiles_per_seq = seq // TM
    const2 = lambda i: (0, 0)
    row = lambda i: (i, 0)
    grp = lambda i: (0, i, 0)
    grp4 = lambda i: (0, i, 0, 0)
    pos = lambda i: (i % tiles_per_seq, 0)
    ng = N_KV_GROUPS
    return pl.pallas_call(
        functools.partial(_inproj_body, tiles_per_seq=tiles_per_seq),
        grid=(t // TM,),
        in_specs=[
            pl.BlockSpec((TM, d), row),
            pl.BlockSpec((1, d), const2),
            pl.BlockSpec((d, _IN_COLS), const2, pipeline_mode=pl.Buffered(1)),
            pl.BlockSpec((TM, LANES), pos),
            pl.BlockSpec((TM, LANES), pos),
            pl.BlockSpec((1, SGU_W), const2),
            pl.BlockSpec((SGU_GROUPS, CHUNK, CHUNK), lambda i: (0, 0, 0)),
            pl.BlockSpec((CHUNK, SGU_W), const2),
        ],
        out_specs=[
            pl.BlockSpec((TM, Q_W), row),
            pl.BlockSpec((TM, KV_W), row),
            pl.BlockSpec((TM, KV_W), row),
            pl.BlockSpec((ng, TM, 2 * LANES), grp),
            pl.BlockSpec((ng, TM, 2 * LANES), grp),
            pl.BlockSpec((ng, 1, V_ROWS, TK), grp4),
            pl.BlockSpec((ng, TM // TQ, V_ROWS, TQ), grp4),
            pl.BlockSpec((ng, TM, LANES), grp),
            pl.BlockSpec((ng, TM, LANES), grp),
            pl.BlockSpec((ng, TM // TQ, V_ROWS, TQ), grp4),
            pl.BlockSpec((TM, 2 * LANES), row),
            pl.BlockSpec((TM, SGU_W), row),
        ],
        out_shape=[
            jax.ShapeDtypeStruct((t, Q_W), BF16),
            jax.ShapeDtypeStruct((t, KV_W), F32),
            jax.ShapeDtypeStruct((t, KV_W), F32),
            jax.ShapeDtypeStruct((ng, t, 2 * LANES), BF16),
            jax.ShapeDtypeStruct((ng, t, 2 * LANES), BF16),
            jax.ShapeDtypeStruct((ng, t // TK, V_ROWS, TK), BF16),
            jax.ShapeDtypeStruct((ng, t // TQ, V_ROWS, TQ), BF16),
            jax.ShapeDtypeStruct((ng, t, LANES), BF16),
            jax.ShapeDtypeStruct((ng, t, LANES), BF16),
            jax.ShapeDtypeStruct((ng, t // TQ, V_ROWS, TQ), BF16),
            jax.ShapeDtypeStruct((t, 2 * LANES), F32),
            jax.ShapeDtypeStruct((t, SGU_W), BF16),
        ],
        compiler_params=_params("parallel"),
        name="inproj",
    )(x2, g, w, cos, sin, sgn, ws, bs)


def _compress_body(kc_ref, vc_ref, pekt_ref, pekb_ref, pevt_ref, pevb_ref,
                   wkt_ref, wkb_ref, wvt_ref, wvb_ref, w2k_ref, w2vt_ref, cos_ref, sin_ref,
                   kca_ref, kcb_ref, vct_ref):
    n_rows = kc_ref.shape[0]

    def hidden(c_ref, pet_ref, peb_ref, wt_ref, wb_ref):
        c = c_ref[...]
        top = _dot((c + pet_ref[...]).astype(BF16), wt_ref[...])
        bot = _dot((c + peb_ref[...]).astype(BF16), wb_ref[...])
        return _gelu_tanh(top + pltpu.roll(bot, n_rows - 1, 0))

    hk = hidden(kc_ref, pekt_ref, pekb_ref, wkt_ref, wkb_ref)
    hv = hidden(vc_ref, pevt_ref, pevb_ref, wvt_ref, wvb_ref)
    cos = cos_ref[...]
    sin = sin_ref[...]
    for g in range(N_KV_GROUPS):
        hkg = hk[:, g * PHI_HIDDEN:(g + 1) * PHI_HIDDEN].astype(BF16)
        hvg = hv[:, g * PHI_HIDDEN:(g + 1) * PHI_HIDDEN].astype(BF16)
        kk = _dot(hkg, w2k_ref[...])
        kca_ref[g] = (kk[:, 0:LANES] * cos + kk[:, 2 * LANES:3 * LANES] * sin).astype(BF16)
        kcb_ref[g] = (kk[:, LANES:2 * LANES] * cos + kk[:, 3 * LANES:4 * LANES] * sin).astype(BF16)
        vct_ref[g, 0] = _dot_nt(w2vt_ref[...], hvg).astype(BF16)


def _compress(kc_rows, vc_rows, pe, w1, w2k, w2vt, cos_c, sin_c, batch):
    total, width = kc_rows.shape
    n_rows = total // batch
    const = lambda b: (0, 0)
    k_shape = jax.ShapeDtypeStruct((N_KV_GROUPS, total, LANES), BF16)
    k_spec = pl.BlockSpec((N_KV_GROUPS, n_rows, LANES), lambda b: (0, b, 0))
    pe_spec = pl.BlockSpec((1, width), const)
    w1_spec = pl.BlockSpec((width, N_KV_GROUPS * PHI_HIDDEN), const)
    return pl.pallas_call(
        _compress_body,
        grid=(batch,),
        in_specs=[
            pl.BlockSpec((n_rows, width), lambda b: (b, 0)),
            pl.BlockSpec((n_rows, width), lambda b: (b, 0)),
            pe_spec, pe_spec, pe_spec, pe_spec,
            w1_spec, w1_spec, w1_spec, w1_spec,
            pl.BlockSpec((PHI_HIDDEN, 4 * LANES), const),
            pl.BlockSpec((HEAD_DIM, PHI_HIDDEN), const),
            pl.BlockSpec((n_rows, LANES), const),
            pl.BlockSpec((n_rows, LANES), const),
        ],
        out_specs=[k_spec, k_spec,
                   pl.BlockSpec((N_KV_GROUPS, 1, HEAD_DIM, n_rows), lambda b: (0, b, 0, 0))],
        out_shape=[k_shape, k_shape,
                   jax.ShapeDtypeStruct((N_KV_GROUPS, batch, HEAD_DIM, n_rows), BF16)],
        compiler_params=_params("parallel"),
        name="compress",
    )(kc_rows, vc_rows, *pe, *w1, w2k, w2vt, cos_c, sin_c)


def _cmp_body(q_ref, kca_ref, kcb_ref, vct_ref, gn_ref, selt_ref, o_ref, bias_ref):
    n_cmp = kca_ref.shape[1]
    n_col = HEADS_PER_GROUP * TQC
    t0 = pl.program_id(2) * TQC
    lq = jnp.concatenate([q_ref[:, pr * LANES:(pr + 1) * LANES] for pr in range(HEAD_PAIRS)], axis=0)
    s = jnp.concatenate([_dot_nt(kca_ref[0], lq), _dot_nt(kcb_ref[0], lq)], axis=1)
    tok = t0 + jnp.bitwise_and(lax.broadcasted_iota(jnp.int32, (1, n_col), 1), TQC - 1)
    cmp_end = lax.broadcasted_iota(jnp.int32, (n_cmp, 1), 0) * CMP_STRIDE + (L_CMP - 1)
    s = jnp.where(cmp_end <= tok, s, NEG_INF)
    e = jnp.exp2(s - jnp.max(s, axis=0, keepdims=True))
    any_valid = jnp.where(tok >= L_CMP - 1, 1.0, 0.0)
    p = e * (any_valid / jnp.sum(e, axis=0, keepdims=True))
    oc_t = _dot(vct_ref[0, 0], p.astype(BF16))
    psum = p[:, 0:TQC]
    for hh in range(1, HEADS_PER_GROUP):
        psum = psum + p[:, hh * TQC:(hh + 1) * TQC]

    selt = selt_ref[...]
    p_hi = psum.astype(BF16)
    r1 = psum - p_hi.astype(F32)
    p_mid = r1.astype(BF16)
    p_lo = (r1 - p_mid.astype(F32)).astype(BF16)
    imp = _dot(selt, p_hi) + _dot(selt, p_mid) + _dot(selt, p_lo)

    n_blk = imp.shape[0]
    jj = lax.broadcasted_iota(jnp.int32, (n_blk, TQC), 0)
    cur = jnp.right_shift(t0 + lax.broadcasted_iota(jnp.int32, (n_blk, TQC), 1), 6)
    jf = jj.astype(F32)
    taken = -3e38
    val = jnp.where(jj < cur, imp, -FORCE)
    for _ in range(N_TOPK - 1):
        best = jnp.max(val, axis=0, keepdims=True)
        first = jnp.min(jnp.where(val == best, jf, 1e6), axis=0, keepdims=True)
        val = jnp.where(jf == first, taken, val)
    bias_t = jnp.where(jnp.logical_and(val == taken, jj < cur), 0.0, NEG_INF)

    for u in range(TQC // LANES):
        rows = slice(u * LANES, (u + 1) * LANES)
        gn_t = gn_ref[rows, :].T
        bias_ref[0, rows, :] = bias_t[:, rows].T.astype(BF16)
        for pr in range(HEAD_PAIRS):
            c_even = pr * TQC + u * LANES
            c_odd = (HEAD_PAIRS + pr) * TQC + u * LANES
            even = oc_t[:, c_even:c_even + LANES] * gn_t[2 * pr:2 * pr + 1, :]
            odd = oc_t[:, c_odd:c_odd + LANES] * gn_t[2 * pr + 1:2 * pr + 2, :]
            o_ref[rows, pr * LANES:(pr + 1) * LANES] = jnp.concatenate([even, odd], axis=0).T


def _cmp_attention(q, kca, kcb, vct, gn, selt, batch, seq):
    t = q.shape[0]
    n_cmp = kca.shape[1] // batch
    nq = seq // TQC
    k_spec = pl.BlockSpec((1, n_cmp, LANES), lambda b, g, i: (g, b, 0))
    return pl.pallas_call(
        _cmp_body,
        grid=(batch, N_KV_GROUPS, nq),
        in_specs=[
            pl.BlockSpec((TQC, 2 * LANES), lambda b, g, i: (b * nq + i, g)),
            k_spec, k_spec,
            pl.BlockSpec((1, 1, HEAD_DIM, n_cmp), lambda b, g, i: (g, b, 0, 0)),
            pl.BlockSpec((TQC, LANES), lambda b, g, i: (b * nq + i, g)),
            pl.BlockSpec(selt.shape, lambda b, g, i: (0, 0)),
        ],
        out_specs=[
            pl.BlockSpec((TQC, 2 * LANES), lambda b, g, i: (b * nq + i, g)),
            pl.BlockSpec((1, TQC, LANES), lambda b, g, i: (g, b * nq + i, 0)),
        ],
        out_shape=[
            jax.ShapeDtypeStruct((t, Q_W), F32),
            jax.ShapeDtypeStruct((N_KV_GROUPS, t, LANES), BF16),
        ],
        compiler_params=_params("parallel", "parallel", "parallel"),
        name="cmp_attn",
    )(q, kca, kcb, vct, gn, selt)


def _selwin_body(q_ref, bias_ref, ksa_ref, ksb_ref, vst_ref, vsd_ref, kwa_ref, kwb_ref, vwt_ref, gn_ref,
                 oc_ref, o_ref, acc_ref, sa_ref, sb_ref, pa_ref, pb_ref, sw_ref):
    qi = pl.program_id(2)
    t0 = qi * TQ
    n_col = HEADS_PER_GROUP * TQ
    q_pairs = [q_ref[:, pr * LANES:(pr + 1) * LANES] for pr in range(HEAD_PAIRS)]
    lq = jnp.concatenate(q_pairs, axis=0)
    bias = bias_ref[0]
    ls = jnp.concatenate([jnp.concatenate([qp, bias], axis=1) for qp in q_pairs], axis=0)
    tok = t0 + jnp.bitwise_and(lax.broadcasted_iota(jnp.int32, (1, n_col), 1), TQ - 1)
    jd = t0 // TK

    def scores(j):
        k_lo = pl.multiple_of(j * TK, TK)
        s = jnp.concatenate([_dot_nt(ksa_ref[0, pl.ds(k_lo, TK), :], ls),
                             _dot_nt(ksb_ref[0, pl.ds(k_lo, TK), :], ls)], axis=1)
        return s, jnp.max(s, axis=0, keepdims=True)

    def weights(s, mx, m_old):
        m_new = jnp.maximum(m_old, mx)
        return m_new, jnp.exp2(m_old - m_new), jnp.exp2(s - m_new).astype(BF16)

    neg_inf = jnp.full((1, n_col), -jnp.inf, F32)
    acc_ref[...] = jnp.zeros(acc_ref.shape, F32)
    pb_ref[...] = jnp.zeros(pb_ref.shape, BF16)
    sa_ref[...], mx0 = scores(0)
    sb_ref[...], mx1 = scores(1)

    jw = jnp.maximum(qi - W_WIN // TQ, 0)
    k0 = pl.multiple_of(jw * TQ, TQ)
    n_win = W_WIN + TQ
    sw = jnp.concatenate([_dot_nt(kwa_ref[0, pl.ds(k0, n_win), :], lq),
                          _dot_nt(kwb_ref[0, pl.ds(k0, n_win), :], lq)], axis=1)
    dist = tok - (k0 + lax.broadcasted_iota(jnp.int32, (n_win, 1), 0))
    in_window = jnp.bitwise_and(dist, -W_WIN) == 0
    sw = jnp.where(in_window, sw, NEG_INF)
    mx_w = jnp.max(sw, axis=0, keepdims=True)
    sw_ref[...] = sw

    def pair(j, carry):
        m, mx_a, mx_b, a_prev = carry
        m, a_cur, pa_ref[...] = weights(sa_ref[...], mx_a, m)
        sa_ref[...], mx_a = scores(j + 2)
        acc_ref[...] = a_prev * acc_ref[...] + _dot(vst_ref[0, jnp.maximum(j - 1, 0)], pb_ref[...])
        m, a_next, pb_ref[...] = weights(sb_ref[...], mx_b, m)
        sb_ref[...], mx_b = scores(j + 3)
        acc_ref[...] = a_cur * acc_ref[...] + _dot(vst_ref[0, j], pa_ref[...])
        return m, mx_a, mx_b, a_next

    carry = (neg_inf, mx0, mx1, jnp.ones((1, n_col), F32))
    m, mx_a, mx_b, a_prev = lax.fori_loop(0, jd // 2, lambda i, c: pair(2 * i, c), carry)

    jf = 2 * (jd // 2)
    m, a_cur, pa_ref[...] = weights(sa_ref[...], mx_a, m)
    acc_ref[...] = a_prev * acc_ref[...] + _dot(vst_ref[0, jnp.maximum(jf - 1, 0)], pb_ref[...])
    m, a_next, pb_ref[...] = weights(sb_ref[...], mx_b, m)
    acc_ref[...] = a_cur * acc_ref[...] + _dot(vst_ref[0, jf], pa_ref[...])
    kd = pl.multiple_of(t0, TQ)
    sd = jnp.concatenate([_dot_nt(ksa_ref[0, pl.ds(kd, TQ), 0:LANES], lq),
                          _dot_nt(ksb_ref[0, pl.ds(kd, TQ), 0:LANES], lq)], axis=1)
    key_d = t0 + lax.broadcasted_iota(jnp.int32, (TQ, 1), 0)
    own = jnp.logical_and(jnp.right_shift(key_d, 6) == jnp.right_shift(tok, 6), key_d <= tok)
    sd = jnp.where(own, sd, NEG_INF)
    _, a_own, pd = weights(sd, jnp.max(sd, axis=0, keepdims=True), m)
    o_sel = a_next * acc_ref[...] + _dot(vst_ref[0, jf + 1], pb_ref[...])
    o_sel = a_own * o_sel + _dot(vsd_ref[0, qi], pd)
    pw = jnp.exp2(sw_ref[...] - mx_w).astype(BF16)
    o_win = _dot(vwt_ref[0, jw], pw[0:TQ])
    for r in range(1, n_win // TQ):
        o_win = o_win + _dot(vwt_ref[0, jw + r], pw[r * TQ:(r + 1) * TQ])
    os_t = o_sel[:HEAD_DIM] * (1.0 / o_sel[HEAD_DIM:HEAD_DIM + 1])
    ow_t = o_win[:HEAD_DIM] * (1.0 / o_win[HEAD_DIM:HEAD_DIM + 1])

    gn_t = gn_ref[...].T
    for pr in range(HEAD_PAIRS):
        halves = []
        for e in range(2):
            c0 = (e * HEAD_PAIRS + pr) * TQ
            hh = 2 * pr + e
            g_sel = gn_t[HEADS_PER_GROUP + hh:HEADS_PER_GROUP + hh + 1, :]
            g_win = gn_t[2 * HEADS_PER_GROUP + hh:2 * HEADS_PER_GROUP + hh + 1, :]
            halves.append(os_t[:, c0:c0 + TQ] * g_sel + ow_t[:, c0:c0 + TQ] * g_win)
        cols = slice(pr * LANES, (pr + 1) * LANES)
        o_ref[:, cols] = (jnp.concatenate(halves, axis=0).T + oc_ref[:, cols]).astype(BF16)


def _selwin_attention(q, bias, ksa, ksb, vst, vsd, kwa, kwb, vwt, gn, oc, batch, seq):
    t = q.shape[0]
    nq = seq // TQ
    n_col = HEADS_PER_GROUP * TQ
    tile_row = lambda b, g, i: (b * nq + i, g)
    grp = lambda b, g, i: (g, b, 0)
    grp4 = lambda b, g, i: (g, b, 0, 0)
    return pl.pallas_call(
        _selwin_body,
        grid=(batch, N_KV_GROUPS, nq),
        in_specs=[
            pl.BlockSpec((TQ, 2 * LANES), tile_row),
            pl.BlockSpec((1, TQ, LANES), lambda b, g, i: (g, b * nq + i, 0)),
            pl.BlockSpec((1, seq, 2 * LANES), grp),
            pl.BlockSpec((1, seq, 2 * LANES), grp),
            pl.BlockSpec((1, seq // TK, V_ROWS, TK), grp4),
            pl.BlockSpec((1, seq // TQ, V_ROWS, TQ), grp4),
            pl.BlockSpec((1, seq, LANES), grp),
            pl.BlockSpec((1, seq, LANES), grp),
            pl.BlockSpec((1, seq // TQ, V_ROWS, TQ), grp4),
            pl.BlockSpec((TQ, LANES), tile_row),
            pl.BlockSpec((TQ, 2 * LANES), tile_row),
        ],
        out_specs=pl.BlockSpec((TQ, 2 * LANES), tile_row),
        out_shape=jax.ShapeDtypeStruct((t, Q_W), BF16),
        scratch_shapes=[pltpu.VMEM((V_ROWS, n_col), F32),
                        pltpu.VMEM((TK, n_col), F32), pltpu.VMEM((TK, n_col), F32),
                        pltpu.VMEM((TK, n_col), BF16), pltpu.VMEM((TK, n_col), BF16),
                        pltpu.VMEM((W_WIN + TQ, n_col), F32)],
        compiler_params=_params("parallel", "parallel", "arbitrary"),
        name="selwin_attn",
    )(q, bias, ksa, ksb, vst, vsd, kwa, kwb, vwt, gn, oc)


def _merge_body(x_ref, g_ref, wg_ref, ya_ref, yb_ref, pa_ref, pb_ref, wo_ref, o_ref):
    x = x_ref[...]
    d = x.shape[1]
    h = _rms(x, g_ref[...]).astype(BF16)
    gates = jax.nn.sigmoid(_dot(h, wg_ref[...]))
    merged = gates[:, :d] * _dot(ya_ref[...], pa_ref[...]) + gates[:, d:] * _dot(yb_ref[...], pb_ref[...])
    o_ref[...] = x + _dot(merged.astype(BF16), wo_ref[...])


def _merge(x2, g, wg, ya, yb, pa, pb, wo):
    t, d = x2.shape
    const = lambda i: (0, 0)
    row = lambda i: (i, 0)
    return pl.pallas_call(
        _merge_body,
        grid=(t // TM,),
        in_specs=[
            pl.BlockSpec((TM, d), row),
            pl.BlockSpec((1, d), const),
            pl.BlockSpec((d, 2 * d), const),
            pl.BlockSpec((TM, Q_W), row),
            pl.BlockSpec((TM, SGU_W), row),
            pl.BlockSpec((Q_W, d), const),
            pl.BlockSpec((SGU_W, d), const),
            pl.BlockSpec((d, d), const),
        ],
        out_specs=pl.BlockSpec((TM, d), row),
        out_shape=jax.ShapeDtypeStruct((t, d), F32),
        compiler_params=_params("parallel"),
        name="merge",
    )(x2, g, wg, ya, yb, pa, pb, wo)


def _rot_cols(w, heads):
    d = w.shape[0]
    w4 = w.reshape(d, heads, 2, HEAD_DIM // 2)
    return jnp.concatenate([-w4[:, :, 1], w4[:, :, 0]], axis=-1).reshape(d, heads * HEAD_DIM)


def _rope_tables(pos):
    inv = ROPE_THETA ** (-jnp.arange(0, HEAD_DIM, 2, dtype=F32) / HEAD_DIM)
    ang = pos.astype(F32)[:, None] * inv[None, :]
    cos = jnp.concatenate([jnp.cos(ang)] * (2 * LANES // HEAD_DIM), axis=-1)
    sin = jnp.concatenate([jnp.sin(ang)] * (2 * LANES // HEAD_DIM), axis=-1)
    return cos, sin


def _inproj_weight(w_in):
    q = w_in[:, 0:Q_W]
    ks = w_in[:, _OFF_KS:_OFF_KS + KV_W]
    kw = w_in[:, _OFF_KW:_OFF_KW + KV_W]
    gn0 = Q_W + 6 * KV_W
    zero = jnp.zeros((w_in.shape[0], LANES - 3 * HEADS_PER_GROUP), w_in.dtype)
    gn = []
    for g in range(N_KV_GROUPS):
        w_g = w_in[:, gn0 + g * 3 * HEADS_PER_GROUP:gn0 + (g + 1) * 3 * HEADS_PER_GROUP]
        gn += [w_g.reshape(-1, HEADS_PER_GROUP, 3).transpose(0, 2, 1).reshape(-1, 3 * HEADS_PER_GROUP), zero]
    uv0 = gn0 + NSA_GATE_W
    uv = w_in[:, uv0:uv0 + 2 * SGU_W]
    w = jnp.concatenate([w_in[:, :gn0], _rot_cols(q, N_HEADS), _rot_cols(ks, N_KV_GROUPS),
                         _rot_cols(kw, N_KV_GROUPS), *gn, uv], axis=1)
    assert w.shape[1] == _IN_COLS
    return w.astype(BF16), w_in[:, uv0 + 2 * SGU_W:].astype(BF16)


def _compress_params(pe_k, pe_v, w1k, w1v, w2k, w2v):
    half = L_CMP // 2

    def pe_rows(pe):
        rep = jnp.broadcast_to(pe[:, None, :], (L_CMP, N_KV_GROUPS, HEAD_DIM))
        return (rep[:half].reshape(1, -1), rep[half:].reshape(1, -1))

    def w1_halves(w1):
        w = w1.reshape(L_CMP, HEAD_DIM, PHI_HIDDEN)
        zero = jnp.zeros((half, HEAD_DIM, PHI_HIDDEN), w.dtype)
        outs = []
        for part in (w[:half], w[half:]):
            blocks = jnp.stack([jnp.concatenate([part, zero], axis=-1),
                                jnp.concatenate([zero, part], axis=-1)], axis=1)
            outs.append(blocks.reshape(half * N_KV_GROUPS * HEAD_DIM, N_KV_GROUPS * PHI_HIDDEN).astype(BF16))
        return outs

    zero = jnp.zeros_like(w2k)
    w2k_rot = _rot_cols(w2k, 1)
    w2k_all = jnp.concatenate([w2k, zero, zero, w2k, w2k_rot, zero, zero, w2k_rot], axis=1).astype(BF16)
    pkt, pkb = pe_rows(pe_k)
    pvt, pvb = pe_rows(pe_v)
    return (pkt, pkb, pvt, pvb), (*w1_halves(w1k), *w1_halves(w1v)), w2k_all, w2v.T.astype(BF16)


def _sel_map_t(seq):
    n_rows = seq // CMP_STRIDE
    n_cmp = (seq - L_CMP) // CMP_STRIDE + 1
    cmp_start = np.arange(n_rows) * CMP_STRIDE
    sel_start = np.arange(LANES) * L_SEL
    ov = (np.minimum(cmp_start[None, :] + L_CMP, sel_start[:, None] + L_SEL)
          - np.maximum(cmp_start[None, :], sel_start[:, None]))
    m = np.clip(ov, 0, None).astype(np.float32) / L_CMP
    m[:, n_cmp:] = 0.0
    m[seq // L_SEL:, :] = 0.0
    return jnp.asarray(m, dtype=BF16)


def kernel(x, ffn1_norm, ffn1_w_gate_up, ffn1_w_down, mix_norm, w_in, cmp_pos_k, cmp_pos_v,
           phi_k_w1, phi_k_w2, phi_v_w1, phi_v_w2, sgu_norm, sgu_w_s, sgu_b_s, proj_a, proj_b,
           w_out, ffn2_norm, ffn2_w_gate_up, ffn2_w_down, final_norm):
    batch, seq, d = x.shape
    depth = w_in.shape[0]
    assert seq % TM == 0 and seq % TK == 0 and seq >= N_TOPK * L_SEL and seq // L_SEL <= LANES
    assert seq % (2 * TK) == 0 and seq >= W_WIN + TQ and N_KV_GROUPS == 2 and HEADS_PER_GROUP == 4
    t = batch * seq
    n_rows = seq // CMP_STRIDE

    cos, sin = _rope_tables(jnp.arange(seq))
    cos_c, sin_c = _rope_tables(jnp.arange(n_rows) * CMP_STRIDE + (L_CMP - 1))
    selt = _sel_map_t(seq)
    row = lambda v: v.reshape(1, -1)

    x2 = x.reshape(t, d)
    for l in range(depth):
        x2 = _ffn(x2, row(ffn1_norm[l]), ffn1_w_gate_up[l].astype(BF16), ffn1_w_down[l].astype(BF16),
                  row(final_norm), False)

        w_main, w_gates = _inproj_weight(w_in[l])
        bs = jnp.repeat(sgu_b_s[l].T, SGU_HEAD, axis=1)
        (q, kc, vc, ksa, ksb, vst, vsd, kwa, kwb, vwt, gn, yb) = _inproj(
            x2, row(mix_norm[l]), w_main, cos, sin, row(sgu_norm[l]), sgu_w_s[l], bs, seq)

        pe, w1, w2k_all, w2vt = _compress_params(
            cmp_pos_k[l], cmp_pos_v[l], phi_k_w1[l], phi_v_w1[l], phi_k_w2[l], phi_v_w2[l])
        width = CMP_STRIDE * KV_W
        kca, kcb, vct = _compress(kc.reshape(t // CMP_STRIDE, width), vc.reshape(t // CMP_STRIDE, width),
                                  pe, w1, w2k_all, w2vt, cos_c, sin_c, batch)

        o_cmp, bias = _cmp_attention(q, kca, kcb, vct, gn, selt, batch, seq)
        ya = _selwin_attention(q, bias, ksa, ksb, vst, vsd, kwa, kwb, vwt, gn, o_cmp, batch, seq)

        x2 = _merge(x2, row(mix_norm[l]), w_gates, ya, yb,
                    proj_a[l].astype(BF16), proj_b[l].astype(BF16), w_out[l].astype(BF16))

        x2 = _ffn(x2, row(ffn2_norm[l]), ffn2_w_gate_up[l].astype(BF16), ffn2_w_down[l].astype(BF16),
                  row(final_norm), l == depth - 1)
    return x2.reshape(batch, seq, d)
```

```python
import functools

import jax
import jax.numpy as jnp
import numpy as np
from jax import lax
from jax.experimental import pallas as pl
from jax.experimental.pallas import tpu as pltpu

HEAD_DIM = 64
N_HEADS = 8
N_KV_GROUPS = 2
HEADS_PER_GROUP = N_HEADS // N_KV_GROUPS
HEAD_PAIRS = HEADS_PER_GROUP // 2
Q_W = N_HEADS * HEAD_DIM
KV_W = N_KV_GROUPS * HEAD_DIM
NSA_GATE_W = N_HEADS * 3
L_CMP = 32
CMP_STRIDE = 16
L_SEL = 64
N_TOPK = 16
W_WIN = 512
PHI_HIDDEN = 256
SGU_GROUPS = 8
SGU_HEAD = 64
SGU_W = SGU_GROUPS * SGU_HEAD
CHUNK = 128
ROPE_THETA = 10000.0
EPS = 1e-6
NEG_INF = -1e30
FORCE = 1e9
SCALE = HEAD_DIM ** -0.5
LOG2E = float(np.log2(np.e))
V_ROWS = HEAD_DIM + 16

LANES = 128
HALF = LANES // 2
VMEM_LIMIT = 56 * 1024 * 1024

TM = 512
TQ = 256
TQC = 256
TK = 512
FF_CHUNK = 256

F32 = jnp.float32
BF16 = jnp.bfloat16


def _dot(a, b):
    return jnp.dot(a, b, preferred_element_type=F32)


def _dot_nt(a, b):
    return lax.dot_general(a, b, (((1,), (1,)), ((), ())), preferred_element_type=F32)


def _rms(x, g):
    return x * lax.rsqrt(jnp.mean(x * x, axis=-1, keepdims=True) + EPS) * g


def _gelu_tanh(x):
    c = np.float32(np.sqrt(2.0 / np.pi))
    return 0.5 * x * (1.0 + jnp.tanh(c * (x + 0.044715 * (x * x * x))))


def _params(*sem):
    return pltpu.CompilerParams(dimension_semantics=sem, vmem_limit_bytes=VMEM_LIMIT)


def _ffn_body(x_ref, g_ref, wgu_ref, wd_ref, fg_ref, o_ref, *, d_ff, final_norm):
    x = x_ref[...]
    h = _rms(x, g_ref[...]).astype(BF16)
    acc = jnp.zeros(x.shape, F32)
    for c in range(d_ff // FF_CHUNK):
        lo = c * FF_CHUNK
        gate = _dot(h, wgu_ref[:, lo:lo + FF_CHUNK])
        up = _dot(h, wgu_ref[:, d_ff + lo:d_ff + lo + FF_CHUNK])
        act = (gate * jax.nn.sigmoid(gate) * up).astype(BF16)
        acc = acc + _dot(act, wd_ref[lo:lo + FF_CHUNK, :])
    y = x + 0.5 * acc
    if final_norm:
        y = _rms(y, fg_ref[...])
    o_ref[...] = y


def _ffn(x2, g, wgu, wd, fg, final_norm):
    t, d = x2.shape
    d_ff = wd.shape[0]
    assert t % TM == 0 and d_ff % FF_CHUNK == 0
    const = lambda i: (0, 0)
    return pl.pallas_call(
        functools.partial(_ffn_body, d_ff=d_ff, final_norm=final_norm),
        grid=(t // TM,),
        in_specs=[
            pl.BlockSpec((TM, d), lambda i: (i, 0)),
            pl.BlockSpec((1, d), const),
            pl.BlockSpec((d, 2 * d_ff), const, pipeline_mode=pl.Buffered(1)),
            pl.BlockSpec((d_ff, d), const, pipeline_mode=pl.Buffered(1)),
            pl.BlockSpec((1, d), const),
        ],
        out_specs=pl.BlockSpec((TM, d), lambda i: (i, 0)),
        out_shape=jax.ShapeDtypeStruct((t, d), F32),
        compiler_params=_params("parallel"),
        name="ffn",
    )(x2, g, wgu, wd, fg)


_OFF_Q, _OFF_KC, _OFF_VC, _OFF_KS, _OFF_VS, _OFF_KW, _OFF_VW = 0, 512, 640, 768, 896, 1024, 1152
_OFF_RQ, _OFF_RKS, _OFF_RKW, _OFF_GN, _OFF_UV = 1280, 1792, 1920, 2048, 2304
_IN_COLS = _OFF_UV + 2 * SGU_W


def _inproj_body(x_ref, g_ref, w_ref, cos_ref, sin_ref, sgn_ref, ws_ref, bs_ref,
                 q_ref, kc_ref, vc_ref, ksa_ref, ksb_ref, vst_ref, vsd_ref, kwa_ref, kwb_ref, vwt_ref,
                 gn_ref, yb_ref, *, tiles_per_seq):
    h = _rms(x_ref[...], g_ref[...]).astype(BF16)
    p = _dot(h, w_ref[...])
    cos = cos_ref[...]
    sin = sin_ref[...]
    cos4 = jnp.concatenate([cos] * (Q_W // LANES), axis=1)
    sin4 = jnp.concatenate([sin] * (Q_W // LANES), axis=1)

    q = p[:, _OFF_Q:_OFF_Q + Q_W] * cos4 + p[:, _OFF_RQ:_OFF_RQ + Q_W] * sin4
    q_ref[...] = (q * (SCALE * LOG2E)).astype(BF16)
    kc_ref[...] = p[:, _OFF_KC:_OFF_KC + KV_W]
    vc_ref[...] = p[:, _OFF_VC:_OFF_VC + KV_W]

    ksel = p[:, _OFF_KS:_OFF_KS + KV_W] * cos + p[:, _OFF_RKS:_OFF_RKS + KV_W] * sin
    kwin = p[:, _OFF_KW:_OFF_KW + KV_W] * cos + p[:, _OFF_RKW:_OFF_RKW + KV_W] * sin

    shape = ksel.shape
    lane = lax.broadcasted_iota(jnp.int32, shape, 1)
    row = lax.broadcasted_iota(jnp.int32, shape, 0)
    lo = lane < HALF
    pos = (pl.program_id(0) % tiles_per_seq) * TM + row
    onehot = jnp.where(lane == jnp.right_shift(pos, 6), 1.0, 0.0).astype(BF16)
    ksel_sw = pltpu.roll(ksel, HALF, 1)
    kwin_sw = pltpu.roll(kwin, HALF, 1)
    for g, (ks_lo, ks_hi, kw_lo, kw_hi) in enumerate(((ksel, ksel_sw, kwin, kwin_sw),
                                                      (ksel_sw, ksel, kwin_sw, kwin))):
        ksa_ref[g] = jnp.concatenate([jnp.where(lo, ks_lo, 0.0).astype(BF16), onehot], axis=1)
        ksb_ref[g] = jnp.concatenate([jnp.where(lo, 0.0, ks_hi).astype(BF16), onehot], axis=1)
        kwa_ref[g] = jnp.where(lo, kw_lo, 0.0).astype(BF16)
        kwb_ref[g] = jnp.where(lo, 0.0, kw_hi).astype(BF16)

    vsel_t = p[:, _OFF_VS:_OFF_VS + KV_W].T
    vwin_t = p[:, _OFF_VW:_OFF_VW + KV_W].T
    ones_row = jnp.where(lax.broadcasted_iota(jnp.int32, (V_ROWS - HEAD_DIM, TM), 0) == 0, 1.0, 0.0)
    for g in range(N_KV_GROUPS):
        rows = slice(g * HEAD_DIM, (g + 1) * HEAD_DIM)
        vs_aug = jnp.concatenate([vsel_t[rows, :], ones_row], axis=0).astype(BF16)
        vw_aug = jnp.concatenate([vwin_t[rows, :], ones_row], axis=0).astype(BF16)
        vst_ref[g, 0] = vs_aug
        for r in range(TM // TQ):
            vsd_ref[g, r] = vs_aug[:, r * TQ:(r + 1) * TQ]
            vwt_ref[g, r] = vw_aug[:, r * TQ:(r + 1) * TQ]

    gn_ref[...] = jax.nn.sigmoid(p[:, _OFF_GN:_OFF_GN + 2 * LANES])

    ge = _gelu_tanh(p[:, _OFF_UV:_OFF_UV + 2 * SGU_W])
    u = ge[:, :SGU_W]
    v = _rms(ge[:, SGU_W:], sgn_ref[...])
    tri = (lax.broadcasted_iota(jnp.int32, (CHUNK, CHUNK), 0)
           >= lax.broadcasted_iota(jnp.int32, (CHUNK, CHUNK), 1))
    lo_c = lax.broadcasted_iota(jnp.int32, (CHUNK, LANES), 1) < HALF
    n_chunks = TM // CHUNK
    cols = []
    for pr in range(SGU_W // LANES):
        w_even = jnp.where(tri, ws_ref[2 * pr], 0.0).astype(BF16)
        w_odd = jnp.where(tri, ws_ref[2 * pr + 1], 0.0).astype(BF16)
        rows = []
        for r in range(n_chunks):
            vp = v[r * CHUNK:(r + 1) * CHUNK, pr * LANES:(pr + 1) * LANES]
            v_even = jnp.where(lo_c, vp, 0.0).astype(BF16)
            v_odd = jnp.where(lo_c, 0.0, vp).astype(BF16)
            rows.append(_dot(w_even, v_even) + _dot(w_odd, v_odd))
        cols.append(jnp.concatenate(rows, axis=0))
    vo = jnp.concatenate(cols, axis=1)
    bias = jnp.concatenate([bs_ref[...]] * n_chunks, axis=0)
    yb_ref[...] = (u * (vo + bias)).astype(BF16)


def _inproj(x2, g, w, cos, sin, sgn, ws, bs, seq):
    t, d = x2.shape
    assert TM == TK
    tiles_per_seq = seq // TM
    const2 = lambda i: (0, 0)
    row = lambda i: (i, 0)
    grp = lambda i: (0, i, 0)
    grp4 = lambda i: (0, i, 0, 0)
    pos = lambda i: (i % tiles_per_seq, 0)
    ng = N_KV_GROUPS
    return pl.pallas_call(
        functools.partial(_inproj_body, tiles_per_seq=tiles_per_seq),
        grid=(t // TM,),
        in_specs=[
            pl.BlockSpec((TM, d), row),
            pl.BlockSpec((1, d), const2),
            pl.BlockSpec((d, _IN_COLS), const2, pipeline_mode=pl.Buffered(1)),
            pl.BlockSpec((TM, LANES), pos),
            pl.BlockSpec((TM, LANES), pos),
            pl.BlockSpec((1, SGU_W), const2),
            pl.BlockSpec((SGU_GROUPS, CHUNK, CHUNK), lambda i: (0, 0, 0)),
            pl.BlockSpec((CHUNK, SGU_W), const2),
        ],
        out_specs=[
            pl.BlockSpec((TM, Q_W), row),
            pl.BlockSpec((TM, KV_W), row),
            pl.BlockSpec((TM, KV_W), row),
            pl.BlockSpec((ng, TM, 2 * LANES), grp),
            pl.BlockSpec((ng, TM, 2 * LANES), grp),
            pl.BlockSpec((ng, 1, V_ROWS, TK), grp4),
            pl.BlockSpec((ng, TM // TQ, V_ROWS, TQ), grp4),
            pl.BlockSpec((ng, TM, LANES), grp),
            pl.BlockSpec((ng, TM, LANES), grp),
            pl.BlockSpec((ng, TM // TQ, V_ROWS, TQ), grp4),
            pl.BlockSpec((TM, 2 * LANES), row),
            pl.BlockSpec((TM, SGU_W), row),
        ],
        out_shape=[
            jax.ShapeDtypeStruct((t, Q_W), BF16),
            jax.ShapeDtypeStruct((t, KV_W), F32),
            jax.ShapeDtypeStruct((t, KV_W), F32),
            jax.ShapeDtypeStruct((ng, t, 2 * LANES), BF16),
            jax.ShapeDtypeStruct((ng, t, 2 * LANES), BF16),
            jax.ShapeDtypeStruct((ng, t // TK, V_ROWS, TK), BF16),
            jax.ShapeDtypeStruct((ng, t // TQ, V_ROWS, TQ), BF16),
            jax.ShapeDtypeStruct((ng, t, LANES), BF16),
            jax.ShapeDtypeStruct((ng, t, LANES), BF16),
            jax.ShapeDtypeStruct((ng, t // TQ, V_ROWS, TQ), BF16),
            jax.ShapeDtypeStruct((t, 2 * LANES), F32),
            jax.ShapeDtypeStruct((t, SGU_W), BF16),
        ],
        compiler_params=_params("parallel"),
        name="inproj",
    )(x2, g, w, cos, sin, sgn, ws, bs)


def _compress_body(kc_ref, vc_ref, pekt_ref, pekb_ref, pevt_ref, pevb_ref,
                   wkt_ref, wkb_ref, wvt_ref, wvb_ref, w2k_ref, w2vt_ref, cos_ref, sin_ref,
                   kca_ref, kcb_ref, vct_ref):
    n_rows = kc_ref.shape[0]

    def hidden(c_ref, pet_ref, peb_ref, wt_ref, wb_ref):
        c = c_ref[...]
        top = _dot((c + pet_ref[...]).astype(BF16), wt_ref[...])
        bot = _dot((c + peb_ref[...]).astype(BF16), wb_ref[...])
        return _gelu_tanh(top + pltpu.roll(bot, n_rows - 1, 0))

    hk = hidden(kc_ref, pekt_ref, pekb_ref, wkt_ref, wkb_ref)
    hv = hidden(vc_ref, pevt_ref, pevb_ref, wvt_ref, wvb_ref)
    cos = cos_ref[...]
    sin = sin_ref[...]
    for g in range(N_KV_GROUPS):
        hkg = hk[:, g * PHI_HIDDEN:(g + 1) * PHI_HIDDEN].astype(BF16)
        hvg = hv[:, g * PHI_HIDDEN:(g + 1) * PHI_HIDDEN].astype(BF16)
        kk = _dot(hkg, w2k_ref[...])
        kca_ref[g] = (kk[:, 0:LANES] * cos + kk[:, 2 * LANES:3 * LANES] * sin).astype(BF16)
        kcb_ref[g] = (kk[:, LANES:2 * LANES] * cos + kk[:, 3 * LANES:4 * LANES] * sin).astype(BF16)
        vct_ref[g, 0] = _dot_nt(w2vt_ref[...], hvg).astype(BF16)


def _compress(kc_rows, vc_rows, pe, w1, w2k, w2vt, cos_c, sin_c, batch):
    total, width = kc_rows.shape
    n_rows = total // batch
    const = lambda b: (0, 0)
    k_shape = jax.ShapeDtypeStruct((N_KV_GROUPS, total, LANES), BF16)
    k_spec = pl.BlockSpec((N_KV_GROUPS, n_rows, LANES), lambda b: (0, b, 0))
    pe_spec = pl.BlockSpec((1, width), const)
    w1_spec = pl.BlockSpec((width, N_KV_GROUPS * PHI_HIDDEN), const)
    return pl.pallas_call(
        _compress_body,
        grid=(batch,),
        in_specs=[
            pl.BlockSpec((n_rows, width), lambda b: (b, 0)),
            pl.BlockSpec((n_rows, width), lambda b: (b, 0)),
            pe_spec, pe_spec, pe_spec, pe_spec,
            w1_spec, w1_spec, w1_spec, w1_spec,
            pl.BlockSpec((PHI_HIDDEN, 4 * LANES), const),
            pl.BlockSpec((HEAD_DIM, PHI_HIDDEN), const),
            pl.BlockSpec((n_rows, LANES), const),
            pl.BlockSpec((n_rows, LANES), const),
        ],
        out_specs=[k_spec, k_spec,
                   pl.BlockSpec((N_KV_GROUPS, 1, HEAD_DIM, n_rows), lambda b: (0, b, 0, 0))],
        out_shape=[k_shape, k_shape,
                   jax.ShapeDtypeStruct((N_KV_GROUPS, batch, HEAD_DIM, n_rows), BF16)],
        compiler_params=_params("parallel"),
        name="compress",
    )(kc_rows, vc_rows, *pe, *w1, w2k, w2vt, cos_c, sin_c)


CMP_CLASSES = 4


def _cmp_body(q_ref, kca_ref, kcb_ref, vct_ref, gn_ref, selt_ref, o_ref, bias_ref):
    n_cmp = kca_ref.shape[1]
    n_blk = selt_ref.shape[0]
    n_col = HEADS_PER_GROUP * TQC
    qi = pl.program_id(2)
    t0 = qi * TQC
    lq = jnp.concatenate([q_ref[:, pr * LANES:(pr + 1) * LANES] for pr in range(HEAD_PAIRS)], axis=0)
    tok = t0 + jnp.bitwise_and(lax.broadcasted_iota(jnp.int32, (1, n_col), 1), TQC - 1)
    any_valid = jnp.where(tok >= L_CMP - 1, 1.0, 0.0)

    def run(rows_cmp, rows_blk):
        s = jnp.concatenate([_dot_nt(kca_ref[0, 0:rows_cmp, :], lq),
                             _dot_nt(kcb_ref[0, 0:rows_cmp, :], lq)], axis=1)
        cmp_end = lax.broadcasted_iota(jnp.int32, (rows_cmp, 1), 0) * CMP_STRIDE + (L_CMP - 1)
        s = jnp.where(cmp_end <= tok, s, NEG_INF)
        e = jnp.exp2(s - jnp.max(s, axis=0, keepdims=True))
        p = e * (any_valid / jnp.sum(e, axis=0, keepdims=True))
        oc_t = _dot(vct_ref[0, 0, :, 0:rows_cmp], p.astype(BF16))
        psum = p[:, 0:TQC]
        for hh in range(1, HEADS_PER_GROUP):
            psum = psum + p[:, hh * TQC:(hh + 1) * TQC]

        selt = selt_ref[0:rows_blk, 0:rows_cmp]
        p_hi = psum.astype(BF16)
        r1 = psum - p_hi.astype(F32)
        p_mid = r1.astype(BF16)
        p_lo = (r1 - p_mid.astype(F32)).astype(BF16)
        imp = _dot(selt, p_hi) + _dot(selt, p_mid) + _dot(selt, p_lo)

        jj = lax.broadcasted_iota(jnp.int32, (rows_blk, TQC), 0)
        cur = jnp.right_shift(t0 + lax.broadcasted_iota(jnp.int32, (rows_blk, TQC), 1), 6)
        jf = jj.astype(F32)
        taken = -3e38
        val = jnp.where(jj < cur, imp, -FORCE)
        for _ in range(N_TOPK - 1):
            best = jnp.max(val, axis=0, keepdims=True)
            first = jnp.min(jnp.where(val == best, jf, 1e6), axis=0, keepdims=True)
            val = jnp.where(jf == first, taken, val)
        bias_t = jnp.where(jnp.logical_and(val == taken, jj < cur), 0.0, NEG_INF)
        if rows_blk < n_blk:
            bias_t = jnp.concatenate([bias_t, jnp.full((n_blk - rows_blk, TQC), NEG_INF, F32)], axis=0)

        for u in range(TQC // LANES):
            rows = slice(u * LANES, (u + 1) * LANES)
            gn_t = gn_ref[rows, :].T
            bias_ref[0, rows, :] = bias_t[:, rows].T.astype(BF16)
            for pr in range(HEAD_PAIRS):
                c_even = pr * TQC + u * LANES
                c_odd = (HEAD_PAIRS + pr) * TQC + u * LANES
                even = oc_t[:, c_even:c_even + LANES] * gn_t[2 * pr:2 * pr + 1, :]
                odd = oc_t[:, c_odd:c_odd + LANES] * gn_t[2 * pr + 1:2 * pr + 2, :]
                o_ref[rows, pr * LANES:(pr + 1) * LANES] = jnp.concatenate([even, odd], axis=0).T

    tiles_per_class = pl.num_programs(2) // CMP_CLASSES
    cls = qi // tiles_per_class
    for c in range(CMP_CLASSES):
        pl.when(cls == c)(functools.partial(run, n_cmp * (c + 1) // CMP_CLASSES, n_blk * (c + 1) // CMP_CLASSES))


def _cmp_attention(q, kca, kcb, vct, gn, selt, batch, seq):
    t = q.shape[0]
    n_cmp = kca.shape[1] // batch
    nq = seq // TQC
    k_spec = pl.BlockSpec((1, n_cmp, LANES), lambda b, g, i: (g, b, 0))
    return pl.pallas_call(
        _cmp_body,
        grid=(batch, N_KV_GROUPS, nq),
        in_specs=[
            pl.BlockSpec((TQC, 2 * LANES), lambda b, g, i: (b * nq + i, g)),
            k_spec, k_spec,
            pl.BlockSpec((1, 1, HEAD_DIM, n_cmp), lambda b, g, i: (g, b, 0, 0)),
            pl.BlockSpec((TQC, LANES), lambda b, g, i: (b * nq + i, g)),
            pl.BlockSpec(selt.shape, lambda b, g, i: (0, 0)),
        ],
        out_specs=[
            pl.BlockSpec((TQC, 2 * LANES), lambda b, g, i: (b * nq + i, g)),
            pl.BlockSpec((1, TQC, LANES), lambda b, g, i: (g, b * nq + i, 0)),
        ],
        out_shape=[
            jax.ShapeDtypeStruct((t, Q_W), F32),
            jax.ShapeDtypeStruct((N_KV_GROUPS, t, LANES), BF16),
        ],
        compiler_params=_params("parallel", "parallel", "parallel"),
        name="cmp_attn",
    )(q, kca, kcb, vct, gn, selt)


def _selwin_body(q_ref, bias_ref, ksa_ref, ksb_ref, vst_ref, vsd_ref, kwa_ref, kwb_ref, vwt_ref, gn_ref,
                 oc_ref, o_ref, acc_ref, sa_ref, sb_ref, pa_ref, pb_ref, sw_ref):
    qi = pl.program_id(2)
    t0 = qi * TQ
    n_col = HEADS_PER_GROUP * TQ
    q_pairs = [q_ref[:, pr * LANES:(pr + 1) * LANES] for pr in range(HEAD_PAIRS)]
    lq = jnp.concatenate(q_pairs, axis=0)
    bias = bias_ref[0]
    ls = jnp.concatenate([jnp.concatenate([qp, bias], axis=1) for qp in q_pairs], axis=0)
    tok = t0 + jnp.bitwise_and(lax.broadcasted_iota(jnp.int32, (1, n_col), 1), TQ - 1)
    jd = t0 // TK

    def scores(j):
        k_lo = pl.multiple_of(j * TK, TK)
        s = jnp.concatenate([_dot_nt(ksa_ref[0, pl.ds(k_lo, TK), :], ls),
                             _dot_nt(ksb_ref[0, pl.ds(k_lo, TK), :], ls)], axis=1)
        return s, jnp.max(s, axis=0, keepdims=True)

    def weights(s, mx, m_old):
        m_new = jnp.maximum(m_old, mx)
        return m_new, jnp.exp2(m_old - m_new), jnp.exp2(s - m_new).astype(BF16)

    neg_inf = jnp.full((1, n_col), -jnp.inf, F32)
    acc_ref[...] = jnp.zeros(acc_ref.shape, F32)
    pb_ref[...] = jnp.zeros(pb_ref.shape, BF16)
    sa_ref[...], mx0 = scores(0)
    sb_ref[...], mx1 = scores(1)

    jw = jnp.maximum(qi - W_WIN // TQ, 0)
    k0 = pl.multiple_of(jw * TQ, TQ)
    n_win = W_WIN + TQ
    sw = jnp.concatenate([_dot_nt(kwa_ref[0, pl.ds(k0, n_win), :], lq),
                          _dot_nt(kwb_ref[0, pl.ds(k0, n_win), :], lq)], axis=1)
    dist = tok - (k0 + lax.broadcasted_iota(jnp.int32, (n_win, 1), 0))
    in_window = jnp.bitwise_and(dist, -W_WIN) == 0
    sw = jnp.where(in_window, sw, NEG_INF)
    mx_w = jnp.max(sw, axis=0, keepdims=True)
    sw_ref[...] = sw

    def pair(j, carry):
        m, mx_a, mx_b, a_prev = carry
        m, a_cur, pa_ref[...] = weights(sa_ref[...], mx_a, m)
        sa_ref[...], mx_a = scores(j + 2)
        acc_ref[...] = a_prev * acc_ref[...] + _dot(vst_ref[0, jnp.maximum(j - 1, 0)], pb_ref[...])
        m, a_next, pb_ref[...] = weights(sb_ref[...], mx_b, m)
        sb_ref[...], mx_b = scores(j + 3)
        acc_ref[...] = a_cur * acc_ref[...] + _dot(vst_ref[0, j], pa_ref[...])
        return m, mx_a, mx_b, a_next

    carry = (neg_inf, mx0, mx1, jnp.ones((1, n_col), F32))
    m, mx_a, mx_b, a_prev = lax.fori_loop(0, jd // 2, lambda i, c: pair(2 * i, c), carry)

    jf = 2 * (jd // 2)
    m, a_cur, pa_ref[...] = weights(sa_ref[...], mx_a, m)
    acc_ref[...] = a_prev * acc_ref[...] + _dot(vst_ref[0, jnp.maximum(jf - 1, 0)], pb_ref[...])
    m, a_next, pb_ref[...] = weights(sb_ref[...], mx_b, m)
    acc_ref[...] = a_cur * acc_ref[...] + _dot(vst_ref[0, jf], pa_ref[...])
    kd = pl.multiple_of(t0, TQ)
    sd = jnp.concatenate([_dot_nt(ksa_ref[0, pl.ds(kd, TQ), 0:LANES], lq),
                          _dot_nt(ksb_ref[0, pl.ds(kd, TQ), 0:LANES], lq)], axis=1)
    key_d = t0 + lax.broadcasted_iota(jnp.int32, (TQ, 1), 0)
    own = jnp.logical_and(jnp.right_shift(key_d, 6) == jnp.right_shift(tok, 6), key_d <= tok)
    sd = jnp.where(own, sd, NEG_INF)
    _, a_own, pd = weights(sd, jnp.max(sd, axis=0, keepdims=True), m)
    o_sel = a_next * acc_ref[...] + _dot(vst_ref[0, jf + 1], pb_ref[...])
    o_sel = a_own * o_sel + _dot(vsd_ref[0, qi], pd)
    pw = jnp.exp2(sw_ref[...] - mx_w).astype(BF16)
    o_win = _dot(vwt_ref[0, jw], pw[0:TQ])
    for r in range(1, n_win // TQ):
        o_win = o_win + _dot(vwt_ref[0, jw + r], pw[r * TQ:(r + 1) * TQ])
    os_t = o_sel[:HEAD_DIM] * (1.0 / o_sel[HEAD_DIM:HEAD_DIM + 1])
    ow_t = o_win[:HEAD_DIM] * (1.0 / o_win[HEAD_DIM:HEAD_DIM + 1])

    gn_t = gn_ref[...].T
    for pr in range(HEAD_PAIRS):
        halves = []
        for e in range(2):
            c0 = (e * HEAD_PAIRS + pr) * TQ
            hh = 2 * pr + e
            g_sel = gn_t[HEADS_PER_GROUP + hh:HEADS_PER_GROUP + hh + 1, :]
            g_win = gn_t[2 * HEADS_PER_GROUP + hh:2 * HEADS_PER_GROUP + hh + 1, :]
            halves.append(os_t[:, c0:c0 + TQ] * g_sel + ow_t[:, c0:c0 + TQ] * g_win)
        cols = slice(pr * LANES, (pr + 1) * LANES)
        o_ref[:, cols] = (jnp.concatenate(halves, axis=0).T + oc_ref[:, cols]).astype(BF16)


def _selwin_attention(q, bias, ksa, ksb, vst, vsd, kwa, kwb, vwt, gn, oc, batch, seq):
    t = q.shape[0]
    nq = seq // TQ
    n_col = HEADS_PER_GROUP * TQ
    tile_row = lambda b, g, i: (b * nq + i, g)
    grp = lambda b, g, i: (g, b, 0)
    grp4 = lambda b, g, i: (g, b, 0, 0)
    return pl.pallas_call(
        _selwin_body,
        grid=(batch, N_KV_GROUPS, nq),
        in_specs=[
            pl.BlockSpec((TQ, 2 * LANES), tile_row),
            pl.BlockSpec((1, TQ, LANES), lambda b, g, i: (g, b * nq + i, 0)),
            pl.BlockSpec((1, seq, 2 * LANES), grp),
            pl.BlockSpec((1, seq, 2 * LANES), grp),
            pl.BlockSpec((1, seq // TK, V_ROWS, TK), grp4),
            pl.BlockSpec((1, seq // TQ, V_ROWS, TQ), grp4),
            pl.BlockSpec((1, seq, LANES), grp),
            pl.BlockSpec((1, seq, LANES), grp),
            pl.BlockSpec((1, seq // TQ, V_ROWS, TQ), grp4),
            pl.BlockSpec((TQ, LANES), tile_row),
            pl.BlockSpec((TQ, 2 * LANES), tile_row),
        ],
        out_specs=pl.BlockSpec((TQ, 2 * LANES), tile_row),
        out_shape=jax.ShapeDtypeStruct((t, Q_W), BF16),
        scratch_shapes=[pltpu.VMEM((V_ROWS, n_col), F32),
                        pltpu.VMEM((TK, n_col), F32), pltpu.VMEM((TK, n_col), F32),
                        pltpu.VMEM((TK, n_col), BF16), pltpu.VMEM((TK, n_col), BF16),
                        pltpu.VMEM((W_WIN + TQ, n_col), F32)],
        compiler_params=_params("parallel", "parallel", "arbitrary"),
        name="selwin_attn",
    )(q, bias, ksa, ksb, vst, vsd, kwa, kwb, vwt, gn, oc)


def _merge_body(x_ref, g_ref, wg_ref, ya_ref, yb_ref, pa_ref, pb_ref, wo_ref, o_ref):
    x = x_ref[...]
    d = x.shape[1]
    h = _rms(x, g_ref[...]).astype(BF16)
    gates = jax.nn.sigmoid(_dot(h, wg_ref[...]))
    merged = gates[:, :d] * _dot(ya_ref[...], pa_ref[...]) + gates[:, d:] * _dot(yb_ref[...], pb_ref[...])
    o_ref[...] = x + _dot(merged.astype(BF16), wo_ref[...])


def _merge(x2, g, wg, ya, yb, pa, pb, wo):
    t, d = x2.shape
    const = lambda i: (0, 0)
    row = lambda i: (i, 0)
    return pl.pallas_call(
        _merge_body,
        grid=(t // TM,),
        in_specs=[
            pl.BlockSpec((TM, d), row),
            pl.BlockSpec((1, d), const),
            pl.BlockSpec((d, 2 * d), const),
            pl.BlockSpec((TM, Q_W), row),
            pl.BlockSpec((TM, SGU_W), row),
            pl.BlockSpec((Q_W, d), const),
            pl.BlockSpec((SGU_W, d), const),
            pl.BlockSpec((d, d), const),
        ],
        out_specs=pl.BlockSpec((TM, d), row),
        out_shape=jax.ShapeDtypeStruct((t, d), F32),
        compiler_params=_params("parallel"),
        name="merge",
    )(x2, g, wg, ya, yb, pa, pb, wo)


def _rot_cols(w, heads):
    d = w.shape[0]
    w4 = w.reshape(d, heads, 2, HEAD_DIM // 2)
    return jnp.concatenate([-w4[:, :, 1], w4[:, :, 0]], axis=-1).reshape(d, heads * HEAD_DIM)


def _rope_tables(pos):
    inv = ROPE_THETA ** (-jnp.arange(0, HEAD_DIM, 2, dtype=F32) / HEAD_DIM)
    ang = pos.astype(F32)[:, None] * inv[None, :]
    cos = jnp.concatenate([jnp.cos(ang)] * (2 * LANES // HEAD_DIM), axis=-1)
    sin = jnp.concatenate([jnp.sin(ang)] * (2 * LANES // HEAD_DIM), axis=-1)
    return cos, sin


def _inproj_weight(w_in):
    q = w_in[:, 0:Q_W]
    ks = w_in[:, _OFF_KS:_OFF_KS + KV_W]
    kw = w_in[:, _OFF_KW:_OFF_KW + KV_W]
    gn0 = Q_W + 6 * KV_W
    zero = jnp.zeros((w_in.shape[0], LANES - 3 * HEADS_PER_GROUP), w_in.dtype)
    gn = []
    for g in range(N_KV_GROUPS):
        w_g = w_in[:, gn0 + g * 3 * HEADS_PER_GROUP:gn0 + (g + 1) * 3 * HEADS_PER_GROUP]
        gn += [w_g.reshape(-1, HEADS_PER_GROUP, 3).transpose(0, 2, 1).reshape(-1, 3 * HEADS_PER_GROUP), zero]
    uv0 = gn0 + NSA_GATE_W
    uv = w_in[:, uv0:uv0 + 2 * SGU_W]
    w = jnp.concatenate([w_in[:, :gn0], _rot_cols(q, N_HEADS), _rot_cols(ks, N_KV_GROUPS),
                         _rot_cols(kw, N_KV_GROUPS), *gn, uv], axis=1)
    assert w.shape[1] == _IN_COLS
    return w.astype(BF16), w_in[:, uv0 + 2 * SGU_W:].astype(BF16)


def _compress_params(pe_k, pe_v, w1k, w1v, w2k, w2v):
    half = L_CMP // 2

    def pe_rows(pe):
        rep = jnp.broadcast_to(pe[:, None, :], (L_CMP, N_KV_GROUPS, HEAD_DIM))
        return (rep[:half].reshape(1, -1), rep[half:].reshape(1, -1))

    def w1_halves(w1):
        w = w1.reshape(L_CMP, HEAD_DIM, PHI_HIDDEN)
        zero = jnp.zeros((half, HEAD_DIM, PHI_HIDDEN), w.dtype)
        outs = []
        for part in (w[:half], w[half:]):
            blocks = jnp.stack([jnp.concatenate([part, zero], axis=-1),
                                jnp.concatenate([zero, part], axis=-1)], axis=1)
            outs.append(blocks.reshape(half * N_KV_GROUPS * HEAD_DIM, N_KV_GROUPS * PHI_HIDDEN).astype(BF16))
        return outs

    zero = jnp.zeros_like(w2k)
    w2k_rot = _rot_cols(w2k, 1)
    w2k_all = jnp.concatenate([w2k, zero, zero, w2k, w2k_rot, zero, zero, w2k_rot], axis=1).astype(BF16)
    pkt, pkb = pe_rows(pe_k)
    pvt, pvb = pe_rows(pe_v)
    return (pkt, pkb, pvt, pvb), (*w1_halves(w1k), *w1_halves(w1v)), w2k_all, w2v.T.astype(BF16)


def _sel_map_t(seq):
    n_rows = seq // CMP_STRIDE
    n_cmp = (seq - L_CMP) // CMP_STRIDE + 1
    cmp_start = np.arange(n_rows) * CMP_STRIDE
    sel_start = np.arange(LANES) * L_SEL
    ov = (np.minimum(cmp_start[None, :] + L_CMP, sel_start[:, None] + L_SEL)
          - np.maximum(cmp_start[None, :], sel_start[:, None]))
    m = np.clip(ov, 0, None).astype(np.float32) / L_CMP
    m[:, n_cmp:] = 0.0
    m[seq // L_SEL:, :] = 0.0
    return jnp.asarray(m, dtype=BF16)


def kernel(x, ffn1_norm, ffn1_w_gate_up, ffn1_w_down, mix_norm, w_in, cmp_pos_k, cmp_pos_v,
           phi_k_w1, phi_k_w2, phi_v_w1, phi_v_w2, sgu_norm, sgu_w_s, sgu_b_s, proj_a, proj_b,
           w_out, ffn2_norm, ffn2_w_gate_up, ffn2_w_down, final_norm):
    batch, seq, d = x.shape
    depth = w_in.shape[0]
    assert seq % TM == 0 and seq % TK == 0 and seq >= N_TOPK * L_SEL and seq // L_SEL <= LANES
    assert seq % (2 * TK) == 0 and seq >= W_WIN + TQ and N_KV_GROUPS == 2 and HEADS_PER_GROUP == 4
    t = batch * seq
    n_rows = seq // CMP_STRIDE

    cos, sin = _rope_tables(jnp.arange(seq))
    cos_c, sin_c = _rope_tables(jnp.arange(n_rows) * CMP_STRIDE + (L_CMP - 1))
    selt = _sel_map_t(seq)
    row = lambda v: v.reshape(1, -1)

    x2 = x.reshape(t, d)
    for l in range(depth):
        x2 = _ffn(x2, row(ffn1_norm[l]), ffn1_w_gate_up[l].astype(BF16), ffn1_w_down[l].astype(BF16),
                  row(final_norm), False)

        w_main, w_gates = _inproj_weight(w_in[l])
        bs = jnp.repeat(sgu_b_s[l].T, SGU_HEAD, axis=1)
        (q, kc, vc, ksa, ksb, vst, vsd, kwa, kwb, vwt, gn, yb) = _inproj(
            x2, row(mix_norm[l]), w_main, cos, sin, row(sgu_norm[l]), sgu_w_s[l], bs, seq)

        pe, w1, w2k_all, w2vt = _compress_params(
            cmp_pos_k[l], cmp_pos_v[l], phi_k_w1[l], phi_v_w1[l], phi_k_w2[l], phi_v_w2[l])
        width = CMP_STRIDE * KV_W
        kca, kcb, vct = _compress(kc.reshape(t // CMP_STRIDE, width), vc.reshape(t // CMP_STRIDE, width),
                                  pe, w1, w2k_all, w2vt, cos_c, sin_c, batch)

        o_cmp, bias = _cmp_attention(q, kca, kcb, vct, gn, selt, batch, seq)
        ya = _selwin_attention(q, bias, ksa, ksb, vst, vsd, kwa, kwb, vwt, gn, o_cmp, batch, seq)

        x2 = _merge(x2, row(mix_norm[l]), w_gates, ya, yb,
                    proj_a[l].astype(BF16), proj_b[l].astype(BF16), w_out[l].astype(BF16))

        x2 = _ffn(x2, row(ffn2_norm[l]), ffn2_w_gate_up[l].astype(BF16), ffn2_w_down[l].astype(BF16),
                  row(final_norm), l == depth - 1)
    return x2.reshape(batch, seq, d)
```

```python
import functools

import jax
import jax.numpy as jnp
import numpy as np
from jax import lax
from jax.experimental import pallas as pl
from jax.experimental.pallas import tpu as pltpu

HEAD_DIM = 64
N_HEADS = 8
N_KV_GROUPS = 2
HEADS_PER_GROUP = N_HEADS // N_KV_GROUPS
HEAD_PAIRS = HEADS_PER_GROUP // 2
Q_W = N_HEADS * HEAD_DIM
KV_W = N_KV_GROUPS * HEAD_DIM
NSA_GATE_W = N_HEADS * 3
L_CMP = 32
CMP_STRIDE = 16
L_SEL = 64
N_TOPK = 16
W_WIN = 512
PHI_HIDDEN = 256
SGU_GROUPS = 8
SGU_HEAD = 64
SGU_W = SGU_GROUPS * SGU_HEAD
CHUNK = 128
ROPE_THETA = 10000.0
EPS = 1e-6
NEG_INF = -1e30
FORCE = 1e9
SCALE = HEAD_DIM ** -0.5
LOG2E = float(np.log2(np.e))
V_ROWS = HEAD_DIM + 16

LANES = 128
HALF = LANES // 2
VMEM_LIMIT = 56 * 1024 * 1024

TM = 512
TQ = 256
TQC = 256
TK = 512
SELWIN_TILES = 2
FF_CHUNK = 256

F32 = jnp.float32
BF16 = jnp.bfloat16


def _dot(a, b):
    return jnp.dot(a, b, preferred_element_type=F32)


def _dot_nt(a, b):
    return lax.dot_general(a, b, (((1,), (1,)), ((), ())), preferred_element_type=F32)


def _rms(x, g):
    return x * lax.rsqrt(jnp.mean(x * x, axis=-1, keepdims=True) + EPS) * g


def _gelu_tanh(x):
    c = np.float32(np.sqrt(2.0 / np.pi))
    return 0.5 * x * (1.0 + jnp.tanh(c * (x + 0.044715 * (x * x * x))))


def _params(*sem):
    return pltpu.CompilerParams(dimension_semantics=sem, vmem_limit_bytes=VMEM_LIMIT)


def _ffn_body(x_ref, g_ref, wgu_ref, wd_ref, fg_ref, o_ref, *, d_ff, final_norm):
    x = x_ref[...]
    h = _rms(x, g_ref[...]).astype(BF16)
    acc = jnp.zeros(x.shape, F32)
    for c in range(d_ff // FF_CHUNK):
        lo = c * FF_CHUNK
        gate = _dot(h, wgu_ref[:, lo:lo + FF_CHUNK])
        up = _dot(h, wgu_ref[:, d_ff + lo:d_ff + lo + FF_CHUNK])
        act = (gate * jax.nn.sigmoid(gate) * up).astype(BF16)
        acc = acc + _dot(act, wd_ref[lo:lo + FF_CHUNK, :])
    y = x + 0.5 * acc
    if final_norm:
        y = _rms(y, fg_ref[...])
    o_ref[...] = y


def _ffn(x2, g, wgu, wd, fg, final_norm):
    t, d = x2.shape
    d_ff = wd.shape[0]
    assert t % TM == 0 and d_ff % FF_CHUNK == 0
    const = lambda i: (0, 0)
    return pl.pallas_call(
        functools.partial(_ffn_body, d_ff=d_ff, final_norm=final_norm),
        grid=(t // TM,),
        in_specs=[
            pl.BlockSpec((TM, d), lambda i: (i, 0)),
            pl.BlockSpec((1, d), const),
            pl.BlockSpec((d, 2 * d_ff), const, pipeline_mode=pl.Buffered(1)),
            pl.BlockSpec((d_ff, d), const, pipeline_mode=pl.Buffered(1)),
            pl.BlockSpec((1, d), const),
        ],
        out_specs=pl.BlockSpec((TM, d), lambda i: (i, 0)),
        out_shape=jax.ShapeDtypeStruct((t, d), F32),
        compiler_params=_params("parallel"),
        name="ffn",
    )(x2, g, wgu, wd, fg)


_OFF_Q, _OFF_KC, _OFF_VC, _OFF_KS, _OFF_VS, _OFF_KW, _OFF_VW = 0, 512, 640, 768, 896, 1024, 1152
_OFF_RQ, _OFF_RKS, _OFF_RKW, _OFF_GN, _OFF_UV = 1280, 1792, 1920, 2048, 2304
_IN_COLS = _OFF_UV + 2 * SGU_W


def _inproj_body(x_ref, g_ref, w_ref, cos_ref, sin_ref, sgn_ref, ws_ref, bs_ref,
                 q_ref, kc_ref, vc_ref, ksa_ref, ksb_ref, vst_ref, vsd_ref, kwa_ref, kwb_ref, vwt_ref,
                 gn_ref, yb_ref, *, tiles_per_seq):
    h = _rms(x_ref[...], g_ref[...]).astype(BF16)
    p = _dot(h, w_ref[...])
    cos = cos_ref[...]
    sin = sin_ref[...]
    cos4 = jnp.concatenate([cos] * (Q_W // LANES), axis=1)
    sin4 = jnp.concatenate([sin] * (Q_W // LANES), axis=1)

    q = p[:, _OFF_Q:_OFF_Q + Q_W] * cos4 + p[:, _OFF_RQ:_OFF_RQ + Q_W] * sin4
    q_ref[...] = (q * (SCALE * LOG2E)).astype(BF16)
    kc_ref[...] = p[:, _OFF_KC:_OFF_KC + KV_W]
    vc_ref[...] = p[:, _OFF_VC:_OFF_VC + KV_W]

    ksel = p[:, _OFF_KS:_OFF_KS + KV_W] * cos + p[:, _OFF_RKS:_OFF_RKS + KV_W] * sin
    kwin = p[:, _OFF_KW:_OFF_KW + KV_W] * cos + p[:, _OFF_RKW:_OFF_RKW + KV_W] * sin

    shape = ksel.shape
    lane = lax.broadcasted_iota(jnp.int32, shape, 1)
    row = lax.broadcasted_iota(jnp.int32, shape, 0)
    lo = lane < HALF
    pos = (pl.program_id(0) % tiles_per_seq) * TM + row
    onehot = jnp.where(lane == jnp.right_shift(pos, 6), 1.0, 0.0).astype(BF16)
    ksel_sw = pltpu.roll(ksel, HALF, 1)
    kwin_sw = pltpu.roll(kwin, HALF, 1)
    for g, (ks_lo, ks_hi, kw_lo, kw_hi) in enumerate(((ksel, ksel_sw, kwin, kwin_sw),
                                                      (ksel_sw, ksel, kwin_sw, kwin))):
        ksa_ref[g] = jnp.concatenate([jnp.where(lo, ks_lo, 0.0).astype(BF16), onehot], axis=1)
        ksb_ref[g] = jnp.concatenate([jnp.where(lo, 0.0, ks_hi).astype(BF16), onehot], axis=1)
        kwa_ref[g] = jnp.where(lo, kw_lo, 0.0).astype(BF16)
        kwb_ref[g] = jnp.where(lo, 0.0, kw_hi).astype(BF16)

    vsel_t = p[:, _OFF_VS:_OFF_VS + KV_W].T
    vwin_t = p[:, _OFF_VW:_OFF_VW + KV_W].T
    ones_row = jnp.where(lax.broadcasted_iota(jnp.int32, (V_ROWS - HEAD_DIM, TM), 0) == 0, 1.0, 0.0)
    for g in range(N_KV_GROUPS):
        rows = slice(g * HEAD_DIM, (g + 1) * HEAD_DIM)
        vs_aug = jnp.concatenate([vsel_t[rows, :], ones_row], axis=0).astype(BF16)
        vw_aug = jnp.concatenate([vwin_t[rows, :], ones_row], axis=0).astype(BF16)
        vst_ref[g, 0] = vs_aug
        for r in range(TM // TQ):
            vsd_ref[g, r] = vs_aug[:, r * TQ:(r + 1) * TQ]
            vwt_ref[g, r] = vw_aug[:, r * TQ:(r + 1) * TQ]

    gn_ref[...] = jax.nn.sigmoid(p[:, _OFF_GN:_OFF_GN + 2 * LANES])

    ge = _gelu_tanh(p[:, _OFF_UV:_OFF_UV + 2 * SGU_W])
    u = ge[:, :SGU_W]
    v = _rms(ge[:, SGU_W:], sgn_ref[...])
    tri = (lax.broadcasted_iota(jnp.int32, (CHUNK, CHUNK), 0)
           >= lax.broadcasted_iota(jnp.int32, (CHUNK, CHUNK), 1))
    lo_c = lax.broadcasted_iota(jnp.int32, (CHUNK, LANES), 1) < HALF
    n_chunks = TM // CHUNK
    cols = []
    for pr in range(SGU_W // LANES):
        w_even = jnp.where(tri, ws_ref[2 * pr], 0.0).astype(BF16)
        w_odd = jnp.where(tri, ws_ref[2 * pr + 1], 0.0).astype(BF16)
        rows = []
        for r in range(n_chunks):
            vp = v[r * CHUNK:(r + 1) * CHUNK, pr * LANES:(pr + 1) * LANES]
            v_even = jnp.where(lo_c, vp, 0.0).astype(BF16)
            v_odd = jnp.where(lo_c, 0.0, vp).astype(BF16)
            rows.append(_dot(w_even, v_even) + _dot(w_odd, v_odd))
        cols.append(jnp.concatenate(rows, axis=0))
    vo = jnp.concatenate(cols, axis=1)
    bias = jnp.concatenate([bs_ref[...]] * n_chunks, axis=0)
    yb_ref[...] = (u * (vo + bias)).astype(BF16)


def _inproj(x2, g, w, cos, sin, sgn, ws, bs, seq):
    t, d = x2.shape
    assert TM == TK
    tiles_per_seq = seq // TM
    const2 = lambda i: (0, 0)
    row = lambda i: (i, 0)
    grp = lambda i: (0, i, 0)
    grp4 = lambda i: (0, i, 0, 0)
    pos = lambda i: (i % tiles_per_seq, 0)
    ng = N_KV_GROUPS
    return pl.pallas_call(
        functools.partial(_inproj_body, tiles_per_seq=tiles_per_seq),
        grid=(t // TM,),
        in_specs=[
            pl.BlockSpec((TM, d), row),
            pl.BlockSpec((1, d), const2),
            pl.BlockSpec((d, _IN_COLS), const2, pipeline_mode=pl.Buffered(1)),
            pl.BlockSpec((TM, LANES), pos),
            pl.BlockSpec((TM, LANES), pos),
            pl.BlockSpec((1, SGU_W), const2),
            pl.BlockSpec((SGU_GROUPS, CHUNK, CHUNK), lambda i: (0, 0, 0)),
            pl.BlockSpec((CHUNK, SGU_W), const2),
        ],
        out_specs=[
            pl.BlockSpec((TM, Q_W), row),
            pl.BlockSpec((TM, KV_W), row),
            pl.BlockSpec((TM, KV_W), row),
            pl.BlockSpec((ng, TM, 2 * LANES), grp),
            pl.BlockSpec((ng, TM, 2 * LANES), grp),
            pl.BlockSpec((ng, 1, V_ROWS, TK), grp4),
            pl.BlockSpec((ng, TM // TQ, V_ROWS, TQ), grp4),
            pl.BlockSpec((ng, TM, LANES), grp),
            pl.BlockSpec((ng, TM, LANES), grp),
            pl.BlockSpec((ng, TM // TQ, V_ROWS, TQ), grp4),
            pl.BlockSpec((TM, 2 * LANES), row),
            pl.BlockSpec((TM, SGU_W), row),
        ],
        out_shape=[
            jax.ShapeDtypeStruct((t, Q_W), BF16),
            jax.ShapeDtypeStruct((t, KV_W), F32),
            jax.ShapeDtypeStruct((t, KV_W), F32),
            jax.ShapeDtypeStruct((ng, t, 2 * LANES), BF16),
            jax.ShapeDtypeStruct((ng, t, 2 * LANES), BF16),
            jax.ShapeDtypeStruct((ng, t // TK, V_ROWS, TK), BF16),
            jax.ShapeDtypeStruct((ng, t // TQ, V_ROWS, TQ), BF16),
            jax.ShapeDtypeStruct((ng, t, LANES), BF16),
            jax.ShapeDtypeStruct((ng, t, LANES), BF16),
            jax.ShapeDtypeStruct((ng, t // TQ, V_ROWS, TQ), BF16),
            jax.ShapeDtypeStruct((t, 2 * LANES), F32),
            jax.ShapeDtypeStruct((t, SGU_W), BF16),
        ],
        compiler_params=_params("parallel"),
        name="inproj",
    )(x2, g, w, cos, sin, sgn, ws, bs)


def _compress_body(kc_ref, vc_ref, pekt_ref, pekb_ref, pevt_ref, pevb_ref,
                   wkt_ref, wkb_ref, wvt_ref, wvb_ref, w2k_ref, w2vt_ref, cos_ref, sin_ref,
                   kca_ref, kcb_ref, vct_ref):
    n_rows = kc_ref.shape[0]

    def hidden(c_ref, pet_ref, peb_ref, wt_ref, wb_ref):
        c = c_ref[...]
        top = _dot((c + pet_ref[...]).astype(BF16), wt_ref[...])
        bot = _dot((c + peb_ref[...]).astype(BF16), wb_ref[...])
        return _gelu_tanh(top + pltpu.roll(bot, n_rows - 1, 0))

    hk = hidden(kc_ref, pekt_ref, pekb_ref, wkt_ref, wkb_ref)
    hv = hidden(vc_ref, pevt_ref, pevb_ref, wvt_ref, wvb_ref)
    cos = cos_ref[...]
    sin = sin_ref[...]
    for g in range(N_KV_GROUPS):
        hkg = hk[:, g * PHI_HIDDEN:(g + 1) * PHI_HIDDEN].astype(BF16)
        hvg = hv[:, g * PHI_HIDDEN:(g + 1) * PHI_HIDDEN].astype(BF16)
        kk = _dot(hkg, w2k_ref[...])
        kca_ref[g] = (kk[:, 0:LANES] * cos + kk[:, 2 * LANES:3 * LANES] * sin).astype(BF16)
        kcb_ref[g] = (kk[:, LANES:2 * LANES] * cos + kk[:, 3 * LANES:4 * LANES] * sin).astype(BF16)
        vct_ref[g, 0] = _dot_nt(w2vt_ref[...], hvg).astype(BF16)


def _compress(kc_rows, vc_rows, pe, w1, w2k, w2vt, cos_c, sin_c, batch):
    total, width = kc_rows.shape
    n_rows = total // batch
    const = lambda b: (0, 0)
    k_shape = jax.ShapeDtypeStruct((N_KV_GROUPS, total, LANES), BF16)
    k_spec = pl.BlockSpec((N_KV_GROUPS, n_rows, LANES), lambda b: (0, b, 0))
    pe_spec = pl.BlockSpec((1, width), const)
    w1_spec = pl.BlockSpec((width, N_KV_GROUPS * PHI_HIDDEN), const)
    return pl.pallas_call(
        _compress_body,
        grid=(batch,),
        in_specs=[
            pl.BlockSpec((n_rows, width), lambda b: (b, 0)),
            pl.BlockSpec((n_rows, width), lambda b: (b, 0)),
            pe_spec, pe_spec, pe_spec, pe_spec,
            w1_spec, w1_spec, w1_spec, w1_spec,
            pl.BlockSpec((PHI_HIDDEN, 4 * LANES), const),
            pl.BlockSpec((HEAD_DIM, PHI_HIDDEN), const),
            pl.BlockSpec((n_rows, LANES), const),
            pl.BlockSpec((n_rows, LANES), const),
        ],
        out_specs=[k_spec, k_spec,
                   pl.BlockSpec((N_KV_GROUPS, 1, HEAD_DIM, n_rows), lambda b: (0, b, 0, 0))],
        out_shape=[k_shape, k_shape,
                   jax.ShapeDtypeStruct((N_KV_GROUPS, batch, HEAD_DIM, n_rows), BF16)],
        compiler_params=_params("parallel"),
        name="compress",
    )(kc_rows, vc_rows, *pe, *w1, w2k, w2vt, cos_c, sin_c)


CMP_CLASSES = 4


def _cmp_body(q_ref, kca_ref, kcb_ref, vct_ref, gn_ref, selt_ref, o_ref, bias_ref):
    n_cmp = kca_ref.shape[1]
    n_blk = selt_ref.shape[0]
    n_col = HEADS_PER_GROUP * TQC
    qi = pl.program_id(2)
    t0 = qi * TQC
    lq = jnp.concatenate([q_ref[:, pr * LANES:(pr + 1) * LANES] for pr in range(HEAD_PAIRS)], axis=0)
    tok = t0 + jnp.bitwise_and(lax.broadcasted_iota(jnp.int32, (1, n_col), 1), TQC - 1)
    any_valid = jnp.where(tok >= L_CMP - 1, 1.0, 0.0)

    def run(rows_cmp, rows_blk):
        s = jnp.concatenate([_dot_nt(kca_ref[0, 0:rows_cmp, :], lq),
                             _dot_nt(kcb_ref[0, 0:rows_cmp, :], lq)], axis=1)
        cmp_end = lax.broadcasted_iota(jnp.int32, (rows_cmp, 1), 0) * CMP_STRIDE + (L_CMP - 1)
        s = jnp.where(cmp_end <= tok, s, NEG_INF)
        e = jnp.exp2(s - jnp.max(s, axis=0, keepdims=True))
        p = e * (any_valid / jnp.sum(e, axis=0, keepdims=True))
        oc_t = _dot(vct_ref[0, 0, :, 0:rows_cmp], p.astype(BF16))
        psum = p[:, 0:TQC]
        for hh in range(1, HEADS_PER_GROUP):
            psum = psum + p[:, hh * TQC:(hh + 1) * TQC]

        selt = selt_ref[0:rows_blk, 0:rows_cmp]
        p_hi = psum.astype(BF16)
        r1 = psum - p_hi.astype(F32)
        p_mid = r1.astype(BF16)
        p_lo = (r1 - p_mid.astype(F32)).astype(BF16)
        imp = _dot(selt, p_hi) + _dot(selt, p_mid) + _dot(selt, p_lo)

        jj = lax.broadcasted_iota(jnp.int32, (rows_blk, TQC), 0)
        cur = jnp.right_shift(t0 + lax.broadcasted_iota(jnp.int32, (rows_blk, TQC), 1), 6)
        jf = jj.astype(F32)
        taken = -3e38
        val = jnp.where(jj < cur, imp, -FORCE)
        for _ in range(N_TOPK - 1):
            best = jnp.max(val, axis=0, keepdims=True)
            first = jnp.min(jnp.where(val == best, jf, 1e6), axis=0, keepdims=True)
            val = jnp.where(jf == first, taken, val)
        bias_t = jnp.where(jnp.logical_and(val == taken, jj < cur), 0.0, NEG_INF)
        if rows_blk < n_blk:
            bias_t = jnp.concatenate([bias_t, jnp.full((n_blk - rows_blk, TQC), NEG_INF, F32)], axis=0)

        for u in range(TQC // LANES):
            rows = slice(u * LANES, (u + 1) * LANES)
            gn_t = gn_ref[rows, :].T
            bias_ref[0, rows, :] = bias_t[:, rows].T.astype(BF16)
            for pr in range(HEAD_PAIRS):
                c_even = pr * TQC + u * LANES
                c_odd = (HEAD_PAIRS + pr) * TQC + u * LANES
                even = oc_t[:, c_even:c_even + LANES] * gn_t[2 * pr:2 * pr + 1, :]
                odd = oc_t[:, c_odd:c_odd + LANES] * gn_t[2 * pr + 1:2 * pr + 2, :]
                o_ref[rows, pr * LANES:(pr + 1) * LANES] = jnp.concatenate([even, odd], axis=0).T

    tiles_per_class = pl.num_programs(2) // CMP_CLASSES
    cls = qi // tiles_per_class
    for c in range(CMP_CLASSES):
        pl.when(cls == c)(functools.partial(run, n_cmp * (c + 1) // CMP_CLASSES, n_blk * (c + 1) // CMP_CLASSES))


def _cmp_attention(q, kca, kcb, vct, gn, selt, batch, seq):
    t = q.shape[0]
    n_cmp = kca.shape[1] // batch
    nq = seq // TQC
    k_spec = pl.BlockSpec((1, n_cmp, LANES), lambda b, g, i: (g, b, 0))
    return pl.pallas_call(
        _cmp_body,
        grid=(batch, N_KV_GROUPS, nq),
        in_specs=[
            pl.BlockSpec((TQC, 2 * LANES), lambda b, g, i: (b * nq + i, g)),
            k_spec, k_spec,
            pl.BlockSpec((1, 1, HEAD_DIM, n_cmp), lambda b, g, i: (g, b, 0, 0)),
            pl.BlockSpec((TQC, LANES), lambda b, g, i: (b * nq + i, g)),
            pl.BlockSpec(selt.shape, lambda b, g, i: (0, 0)),
        ],
        out_specs=[
            pl.BlockSpec((TQC, 2 * LANES), lambda b, g, i: (b * nq + i, g)),
            pl.BlockSpec((1, TQC, LANES), lambda b, g, i: (g, b * nq + i, 0)),
        ],
        out_shape=[
            jax.ShapeDtypeStruct((t, Q_W), F32),
            jax.ShapeDtypeStruct((N_KV_GROUPS, t, LANES), BF16),
        ],
        compiler_params=_params("parallel", "parallel", "parallel"),
        name="cmp_attn",
    )(q, kca, kcb, vct, gn, selt)


def _selwin_body(q_ref, bias_ref, ksa_ref, ksb_ref, vst_ref, vsd_ref, kwa_ref, kwb_ref, vwt_ref, gn_ref,
                 oc_ref, tri_ref, o_ref, *scratch):
    per_set = len(scratch) // 2
    for k in range(SELWIN_TILES):
        _selwin_tile(k, q_ref, bias_ref, ksa_ref, ksb_ref, vst_ref, vsd_ref, kwa_ref, kwb_ref, vwt_ref, gn_ref,
                     oc_ref, tri_ref, o_ref, *scratch[(k % 2) * per_set:(k % 2 + 1) * per_set])


def _selwin_tile(k, q_ref, bias_ref, ksa_ref, ksb_ref, vst_ref, vsd_ref, kwa_ref, kwb_ref, vwt_ref, gn_ref,
                 oc_ref, tri_ref, o_ref, acc_ref, sa_ref, sb_ref, pa_ref, pb_ref, sw_ref):
    qi = pl.program_id(2) * SELWIN_TILES + k
    rows = slice(k * TQ, (k + 1) * TQ)
    t0 = qi * TQ
    n_col = HEADS_PER_GROUP * TQ
    q_pairs = [q_ref[rows, pr * LANES:(pr + 1) * LANES] for pr in range(HEAD_PAIRS)]
    lq = jnp.concatenate(q_pairs, axis=0)
    bias = bias_ref[0, rows, :]
    ls = jnp.concatenate([jnp.concatenate([qp, bias], axis=1) for qp in q_pairs], axis=0)
    tok = t0 + jnp.bitwise_and(lax.broadcasted_iota(jnp.int32, (1, n_col), 1), TQ - 1)
    jd = t0 // TK

    def scores(j):
        k_lo = pl.multiple_of(j * TK, TK)
        s = jnp.concatenate([_dot_nt(ksa_ref[0, pl.ds(k_lo, TK), :], ls),
                             _dot_nt(ksb_ref[0, pl.ds(k_lo, TK), :], ls)], axis=1)
        return s, jnp.max(s, axis=0, keepdims=True)

    def weights(s, mx, m_old):
        m_new = jnp.maximum(m_old, mx)
        return m_new, jnp.exp2(m_old - m_new), jnp.exp2(s - m_new).astype(BF16)

    neg_inf = jnp.full((1, n_col), -jnp.inf, F32)
    acc_ref[...] = jnp.zeros(acc_ref.shape, F32)
    pb_ref[...] = jnp.zeros(pb_ref.shape, BF16)
    sa_ref[...], mx0 = scores(0)
    sb_ref[...], mx1 = scores(1)

    n_wt = W_WIN // TQ + 1
    win_tiles = []
    parts = []
    for r in range(n_wt):
        kt = qi - (n_wt - 1) + r
        kt_c = jnp.maximum(kt, 0)
        k_r = pl.multiple_of(kt_c * TQ, TQ)
        s_r = jnp.concatenate([_dot_nt(kwa_ref[0, pl.ds(k_r, TQ), :], lq),
                               _dot_nt(kwb_ref[0, pl.ds(k_r, TQ), :], lq)], axis=1)
        if r == 0:
            s_r = s_r + tri_ref[0]
        if r == n_wt - 1:
            s_r = s_r + tri_ref[1]
        else:
            s_r = jnp.where(kt >= 0, s_r, NEG_INF)
        parts.append(s_r)
        win_tiles.append(kt_c)
    sw = jnp.concatenate(parts, axis=0)
    mx_w = jnp.max(sw, axis=0, keepdims=True)
    sw_ref[...] = sw

    def pair(j, carry):
        m, mx_a, mx_b, a_prev = carry
        m, a_cur, pa_ref[...] = weights(sa_ref[...], mx_a, m)
        sa_ref[...], mx_a = scores(j + 2)
        acc_ref[...] = a_prev * acc_ref[...] + _dot(vst_ref[0, jnp.maximum(j - 1, 0)], pb_ref[...])
        m, a_next, pb_ref[...] = weights(sb_ref[...], mx_b, m)
        sb_ref[...], mx_b = scores(j + 3)
        acc_ref[...] = a_cur * acc_ref[...] + _dot(vst_ref[0, j], pa_ref[...])
        return m, mx_a, mx_b, a_next

    carry = (neg_inf, mx0, mx1, jnp.ones((1, n_col), F32))
    m, mx_a, mx_b, a_prev = lax.fori_loop(0, jd // 2, lambda i, c: pair(2 * i, c), carry)

    jf = 2 * (jd // 2)
    m, a_cur, pa_ref[...] = weights(sa_ref[...], mx_a, m)
    acc_ref[...] = a_prev * acc_ref[...] + _dot(vst_ref[0, jnp.maximum(jf - 1, 0)], pb_ref[...])
    m, a_next, pb_ref[...] = weights(sb_ref[...], mx_b, m)
    acc_ref[...] = a_cur * acc_ref[...] + _dot(vst_ref[0, jf], pa_ref[...])
    kd = pl.multiple_of(t0, TQ)
    sd = jnp.concatenate([_dot_nt(ksa_ref[0, pl.ds(kd, TQ), 0:LANES], lq),
                          _dot_nt(ksb_ref[0, pl.ds(kd, TQ), 0:LANES], lq)], axis=1)
    key_d = t0 + lax.broadcasted_iota(jnp.int32, (TQ, 1), 0)
    own = jnp.logical_and(jnp.right_shift(key_d, 6) == jnp.right_shift(tok, 6), key_d <= tok)
    sd = jnp.where(own, sd, NEG_INF)
    _, a_own, pd = weights(sd, jnp.max(sd, axis=0, keepdims=True), m)
    o_sel = a_next * acc_ref[...] + _dot(vst_ref[0, jf + 1], pb_ref[...])
    o_sel = a_own * o_sel + _dot(vsd_ref[0, qi], pd)
    pw = jnp.exp2(sw_ref[...] - mx_w).astype(BF16)
    o_win = _dot(vwt_ref[0, win_tiles[0]], pw[0:TQ])
    for r in range(1, n_wt):
        o_win = o_win + _dot(vwt_ref[0, win_tiles[r]], pw[r * TQ:(r + 1) * TQ])
    os_t = o_sel[:HEAD_DIM] * (1.0 / o_sel[HEAD_DIM:HEAD_DIM + 1])
    ow_t = o_win[:HEAD_DIM] * (1.0 / o_win[HEAD_DIM:HEAD_DIM + 1])

    gn_t = gn_ref[rows, :].T
    for pr in range(HEAD_PAIRS):
        halves = []
        for e in range(2):
            c0 = (e * HEAD_PAIRS + pr) * TQ
            hh = 2 * pr + e
            g_sel = gn_t[HEADS_PER_GROUP + hh:HEADS_PER_GROUP + hh + 1, :]
            g_win = gn_t[2 * HEADS_PER_GROUP + hh:2 * HEADS_PER_GROUP + hh + 1, :]
            halves.append(os_t[:, c0:c0 + TQ] * g_sel + ow_t[:, c0:c0 + TQ] * g_win)
        cols = slice(pr * LANES, (pr + 1) * LANES)
        o_ref[rows, cols] = (jnp.concatenate(halves, axis=0).T + oc_ref[rows, cols]).astype(BF16)


def _selwin_attention(q, bias, ksa, ksb, vst, vsd, kwa, kwb, vwt, gn, oc, tri, batch, seq):
    t = q.shape[0]
    rows = SELWIN_TILES * TQ
    nq = seq // rows
    n_col = HEADS_PER_GROUP * TQ
    tile_row = lambda b, g, i: (b * nq + i, g)
    grp = lambda b, g, i: (g, b, 0)
    grp4 = lambda b, g, i: (g, b, 0, 0)
    once = pl.Buffered(1)
    scratch_set = [pltpu.VMEM((V_ROWS, n_col), F32),
                   pltpu.VMEM((TK, n_col), F32), pltpu.VMEM((TK, n_col), F32),
                   pltpu.VMEM((TK, n_col), BF16), pltpu.VMEM((TK, n_col), BF16),
                   pltpu.VMEM((W_WIN + TQ, n_col), F32)]
    return pl.pallas_call(
        _selwin_body,
        grid=(batch, N_KV_GROUPS, nq),
        in_specs=[
            pl.BlockSpec((rows, 2 * LANES), tile_row),
            pl.BlockSpec((1, rows, LANES), lambda b, g, i: (g, b * nq + i, 0)),
            pl.BlockSpec((1, seq, 2 * LANES), grp, pipeline_mode=once),
            pl.BlockSpec((1, seq, 2 * LANES), grp, pipeline_mode=once),
            pl.BlockSpec((1, seq // TK, V_ROWS, TK), grp4, pipeline_mode=once),
            pl.BlockSpec((1, seq // TQ, V_ROWS, TQ), grp4, pipeline_mode=once),
            pl.BlockSpec((1, seq, LANES), grp, pipeline_mode=once),
            pl.BlockSpec((1, seq, LANES), grp, pipeline_mode=once),
            pl.BlockSpec((1, seq // TQ, V_ROWS, TQ), grp4, pipeline_mode=once),
            pl.BlockSpec((rows, LANES), tile_row),
            pl.BlockSpec((rows, 2 * LANES), tile_row),
            pl.BlockSpec(tri.shape, lambda b, g, i: (0, 0, 0), pipeline_mode=once),
        ],
        out_specs=pl.BlockSpec((rows, 2 * LANES), tile_row),
        out_shape=jax.ShapeDtypeStruct((t, Q_W), BF16),
        scratch_shapes=scratch_set + scratch_set,
        compiler_params=_params("parallel", "parallel", "arbitrary"),
        name="selwin_attn",
    )(q, bias, ksa, ksb, vst, vsd, kwa, kwb, vwt, gn, oc, tri)


def _merge_body(x_ref, g_ref, wg_ref, ya_ref, yb_ref, pa_ref, pb_ref, wo_ref, o_ref):
    x = x_ref[...]
    d = x.shape[1]
    h = _rms(x, g_ref[...]).astype(BF16)
    gates = jax.nn.sigmoid(_dot(h, wg_ref[...]))
    merged = gates[:, :d] * _dot(ya_ref[...], pa_ref[...]) + gates[:, d:] * _dot(yb_ref[...], pb_ref[...])
    o_ref[...] = x + _dot(merged.astype(BF16), wo_ref[...])


def _merge(x2, g, wg, ya, yb, pa, pb, wo):
    t, d = x2.shape
    const = lambda i: (0, 0)
    row = lambda i: (i, 0)
    return pl.pallas_call(
        _merge_body,
        grid=(t // TM,),
        in_specs=[
            pl.BlockSpec((TM, d), row),
            pl.BlockSpec((1, d), const),
            pl.BlockSpec((d, 2 * d), const),
            pl.BlockSpec((TM, Q_W), row),
            pl.BlockSpec((TM, SGU_W), row),
            pl.BlockSpec((Q_W, d), const),
            pl.BlockSpec((SGU_W, d), const),
            pl.BlockSpec((d, d), const),
        ],
        out_specs=pl.BlockSpec((TM, d), row),
        out_shape=jax.ShapeDtypeStruct((t, d), F32),
        compiler_params=_params("parallel"),
        name="merge",
    )(x2, g, wg, ya, yb, pa, pb, wo)


def _rot_cols(w, heads):
    d = w.shape[0]
    w4 = w.reshape(d, heads, 2, HEAD_DIM // 2)
    return jnp.concatenate([-w4[:, :, 1], w4[:, :, 0]], axis=-1).reshape(d, heads * HEAD_DIM)


def _rope_tables(pos):
    inv = ROPE_THETA ** (-jnp.arange(0, HEAD_DIM, 2, dtype=F32) / HEAD_DIM)
    ang = pos.astype(F32)[:, None] * inv[None, :]
    cos = jnp.concatenate([jnp.cos(ang)] * (2 * LANES // HEAD_DIM), axis=-1)
    sin = jnp.concatenate([jnp.sin(ang)] * (2 * LANES // HEAD_DIM), axis=-1)
    return cos, sin


def _inproj_weight(w_in):
    q = w_in[:, 0:Q_W]
    ks = w_in[:, _OFF_KS:_OFF_KS + KV_W]
    kw = w_in[:, _OFF_KW:_OFF_KW + KV_W]
    gn0 = Q_W + 6 * KV_W
    zero = jnp.zeros((w_in.shape[0], LANES - 3 * HEADS_PER_GROUP), w_in.dtype)
    gn = []
    for g in range(N_KV_GROUPS):
        w_g = w_in[:, gn0 + g * 3 * HEADS_PER_GROUP:gn0 + (g + 1) * 3 * HEADS_PER_GROUP]
        gn += [w_g.reshape(-1, HEADS_PER_GROUP, 3).transpose(0, 2, 1).reshape(-1, 3 * HEADS_PER_GROUP), zero]
    uv0 = gn0 + NSA_GATE_W
    uv = w_in[:, uv0:uv0 + 2 * SGU_W]
    w = jnp.concatenate([w_in[:, :gn0], _rot_cols(q, N_HEADS), _rot_cols(ks, N_KV_GROUPS),
                         _rot_cols(kw, N_KV_GROUPS), *gn, uv], axis=1)
    assert w.shape[1] == _IN_COLS
    return w.astype(BF16), w_in[:, uv0 + 2 * SGU_W:].astype(BF16)


def _compress_params(pe_k, pe_v, w1k, w1v, w2k, w2v):
    half = L_CMP // 2

    def pe_rows(pe):
        rep = jnp.broadcast_to(pe[:, None, :], (L_CMP, N_KV_GROUPS, HEAD_DIM))
        return (rep[:half].reshape(1, -1), rep[half:].reshape(1, -1))

    def w1_halves(w1):
        w = w1.reshape(L_CMP, HEAD_DIM, PHI_HIDDEN)
        zero = jnp.zeros((half, HEAD_DIM, PHI_HIDDEN), w.dtype)
        outs = []
        for part in (w[:half], w[half:]):
            blocks = jnp.stack([jnp.concatenate([part, zero], axis=-1),
                                jnp.concatenate([zero, part], axis=-1)], axis=1)
            outs.append(blocks.reshape(half * N_KV_GROUPS * HEAD_DIM, N_KV_GROUPS * PHI_HIDDEN).astype(BF16))
        return outs

    zero = jnp.zeros_like(w2k)
    w2k_rot = _rot_cols(w2k, 1)
    w2k_all = jnp.concatenate([w2k, zero, zero, w2k, w2k_rot, zero, zero, w2k_rot], axis=1).astype(BF16)
    pkt, pkb = pe_rows(pe_k)
    pvt, pvb = pe_rows(pe_v)
    return (pkt, pkb, pvt, pvb), (*w1_halves(w1k), *w1_halves(w1v)), w2k_all, w2v.T.astype(BF16)


def _window_triangles():
    key = np.arange(TQ)[:, None]
    tok = np.arange(HEADS_PER_GROUP * TQ)[None, :] % TQ
    upper = np.where(key > tok, 0.0, NEG_INF)
    lower = np.where(key <= tok, 0.0, NEG_INF)
    return jnp.asarray(np.stack([upper, lower]), dtype=F32)


def _sel_map_t(seq):
    n_rows = seq // CMP_STRIDE
    n_cmp = (seq - L_CMP) // CMP_STRIDE + 1
    cmp_start = np.arange(n_rows) * CMP_STRIDE
    sel_start = np.arange(LANES) * L_SEL
    ov = (np.minimum(cmp_start[None, :] + L_CMP, sel_start[:, None] + L_SEL)
          - np.maximum(cmp_start[None, :], sel_start[:, None]))
    m = np.clip(ov, 0, None).astype(np.float32) / L_CMP
    m[:, n_cmp:] = 0.0
    m[seq // L_SEL:, :] = 0.0
    return jnp.asarray(m, dtype=BF16)


def kernel(x, ffn1_norm, ffn1_w_gate_up, ffn1_w_down, mix_norm, w_in, cmp_pos_k, cmp_pos_v,
           phi_k_w1, phi_k_w2, phi_v_w1, phi_v_w2, sgu_norm, sgu_w_s, sgu_b_s, proj_a, proj_b,
           w_out, ffn2_norm, ffn2_w_gate_up, ffn2_w_down, final_norm):
    batch, seq, d = x.shape
    depth = w_in.shape[0]
    assert seq % TM == 0 and seq % TK == 0 and seq >= N_TOPK * L_SEL and seq // L_SEL <= LANES
    assert seq % (2 * TK) == 0 and seq >= W_WIN + TQ and W_WIN % TQ == 0 and seq % (SELWIN_TILES * TQ) == 0 and N_KV_GROUPS == 2 and HEADS_PER_GROUP == 4
    t = batch * seq
    n_rows = seq // CMP_STRIDE

    cos, sin = _rope_tables(jnp.arange(seq))
    cos_c, sin_c = _rope_tables(jnp.arange(n_rows) * CMP_STRIDE + (L_CMP - 1))
    selt = _sel_map_t(seq)
    tri = _window_triangles()
    row = lambda v: v.reshape(1, -1)

    x2 = x.reshape(t, d)
    for l in range(depth):
        x2 = _ffn(x2, row(ffn1_norm[l]), ffn1_w_gate_up[l].astype(BF16), ffn1_w_down[l].astype(BF16),
                  row(final_norm), False)

        w_main, w_gates = _inproj_weight(w_in[l])
        bs = jnp.repeat(sgu_b_s[l].T, SGU_HEAD, axis=1)
        (q, kc, vc, ksa, ksb, vst, vsd, kwa, kwb, vwt, gn, yb) = _inproj(
            x2, row(mix_norm[l]), w_main, cos, sin, row(sgu_norm[l]), sgu_w_s[l], bs, seq)

        pe, w1, w2k_all, w2vt = _compress_params(
            cmp_pos_k[l], cmp_pos_v[l], phi_k_w1[l], phi_v_w1[l], phi_k_w2[l], phi_v_w2[l])
        width = CMP_STRIDE * KV_W
        kca, kcb, vct = _compress(kc.reshape(t // CMP_STRIDE, width), vc.reshape(t // CMP_STRIDE, width),
                                  pe, w1, w2k_all, w2vt, cos_c, sin_c, batch)

        o_cmp, bias = _cmp_attention(q, kca, kcb, vct, gn, selt, batch, seq)
        ya = _selwin_attention(q, bias, ksa, ksb, vst, vsd, kwa, kwb, vwt, gn, o_cmp, tri, batch, seq)

        x2 = _merge(x2, row(mix_norm[l]), w_gates, ya, yb,
                    proj_a[l].astype(BF16), proj_b[l].astype(BF16), w_out[l].astype(BF16))

        x2 = _ffn(x2, row(ffn2_norm[l]), ffn2_w_gate_up[l].astype(BF16), ffn2_w_down[l].astype(BF16),
                  row(final_norm), l == depth - 1)
    return x2.reshape(batch, seq, d)
```

```python
import functools

import jax
import jax.numpy as jnp
import numpy as np
from jax import lax
from jax.experimental import pallas as pl
from jax.experimental.pallas import tpu as pltpu

HEAD_DIM = 64
N_HEADS = 8
N_KV_GROUPS = 2
HEADS_PER_GROUP = N_HEADS // N_KV_GROUPS
HEAD_PAIRS = HEADS_PER_GROUP // 2
Q_W = N_HEADS * HEAD_DIM
KV_W = N_KV_GROUPS * HEAD_DIM
NSA_GATE_W = N_HEADS * 3
L_CMP = 32
CMP_STRIDE = 16
L_SEL = 64
N_TOPK = 16
W_WIN = 512
PHI_HIDDEN = 256
SGU_GROUPS = 8
SGU_HEAD = 64
SGU_W = SGU_GROUPS * SGU_HEAD
CHUNK = 128
ROPE_THETA = 10000.0
EPS = 1e-6
NEG_INF = -1e30
FORCE = 1e9
SCALE = HEAD_DIM ** -0.5
LOG2E = float(np.log2(np.e))
V_ROWS = HEAD_DIM + 16

LANES = 128
HALF = LANES // 2
VMEM_LIMIT = 56 * 1024 * 1024

TM = 512
TQ = 256
TQC = 256
TK = 512
SELWIN_TILES = 4
FF_CHUNK = 256

F32 = jnp.float32
BF16 = jnp.bfloat16


def _dot(a, b):
    return jnp.dot(a, b, preferred_element_type=F32)


def _dot_nt(a, b):
    return lax.dot_general(a, b, (((1,), (1,)), ((), ())), preferred_element_type=F32)


def _rms(x, g):
    return x * lax.rsqrt(jnp.mean(x * x, axis=-1, keepdims=True) + EPS) * g


def _gelu_tanh(x):
    c = np.float32(np.sqrt(2.0 / np.pi))
    return 0.5 * x * (1.0 + jnp.tanh(c * (x + 0.044715 * (x * x * x))))


def _params(*sem):
    return pltpu.CompilerParams(dimension_semantics=sem, vmem_limit_bytes=VMEM_LIMIT)


def _ffn_body(x_ref, g_ref, wgu_ref, wd_ref, fg_ref, o_ref, *, d_ff, final_norm):
    x = x_ref[...]
    h = _rms(x, g_ref[...]).astype(BF16)
    acc = jnp.zeros(x.shape, F32)
    for c in range(d_ff // FF_CHUNK):
        lo = c * FF_CHUNK
        gate = _dot(h, wgu_ref[:, lo:lo + FF_CHUNK])
        up = _dot(h, wgu_ref[:, d_ff + lo:d_ff + lo + FF_CHUNK])
        act = (gate * jax.nn.sigmoid(gate) * up).astype(BF16)
        acc = acc + _dot(act, wd_ref[lo:lo + FF_CHUNK, :])
    y = x + 0.5 * acc
    if final_norm:
        y = _rms(y, fg_ref[...])
    o_ref[...] = y


def _ffn(x2, g, wgu, wd, fg, final_norm):
    t, d = x2.shape
    d_ff = wd.shape[0]
    assert t % TM == 0 and d_ff % FF_CHUNK == 0
    const = lambda i: (0, 0)
    return pl.pallas_call(
        functools.partial(_ffn_body, d_ff=d_ff, final_norm=final_norm),
        grid=(t // TM,),
        in_specs=[
            pl.BlockSpec((TM, d), lambda i: (i, 0)),
            pl.BlockSpec((1, d), const),
            pl.BlockSpec((d, 2 * d_ff), const, pipeline_mode=pl.Buffered(1)),
            pl.BlockSpec((d_ff, d), const, pipeline_mode=pl.Buffered(1)),
            pl.BlockSpec((1, d), const),
        ],
        out_specs=pl.BlockSpec((TM, d), lambda i: (i, 0)),
        out_shape=jax.ShapeDtypeStruct((t, d), F32),
        compiler_params=_params("parallel"),
        name="ffn",
    )(x2, g, wgu, wd, fg)


_OFF_Q, _OFF_KC, _OFF_VC, _OFF_KS, _OFF_VS, _OFF_KW, _OFF_VW = 0, 512, 640, 768, 896, 1024, 1152
_OFF_RQ, _OFF_RKS, _OFF_RKW, _OFF_GN, _OFF_UV = 1280, 1792, 1920, 2048, 2304
_IN_COLS = _OFF_UV + 2 * SGU_W


def _inproj_body(x_ref, g_ref, w_ref, cos_ref, sin_ref, sgn_ref, ws_ref, bs_ref,
                 q_ref, kc_ref, vc_ref, ksa_ref, ksb_ref, vst_ref, vsd_ref, kwa_ref, kwb_ref, vwt_ref,
                 gn_ref, yb_ref, *, tiles_per_seq):
    h = _rms(x_ref[...], g_ref[...]).astype(BF16)
    p = _dot(h, w_ref[...])
    cos = cos_ref[...]
    sin = sin_ref[...]
    cos4 = jnp.concatenate([cos] * (Q_W // LANES), axis=1)
    sin4 = jnp.concatenate([sin] * (Q_W // LANES), axis=1)

    q = p[:, _OFF_Q:_OFF_Q + Q_W] * cos4 + p[:, _OFF_RQ:_OFF_RQ + Q_W] * sin4
    q_ref[...] = (q * (SCALE * LOG2E)).astype(BF16)
    kc_ref[...] = p[:, _OFF_KC:_OFF_KC + KV_W]
    vc_ref[...] = p[:, _OFF_VC:_OFF_VC + KV_W]

    ksel = p[:, _OFF_KS:_OFF_KS + KV_W] * cos + p[:, _OFF_RKS:_OFF_RKS + KV_W] * sin
    kwin = p[:, _OFF_KW:_OFF_KW + KV_W] * cos + p[:, _OFF_RKW:_OFF_RKW + KV_W] * sin

    shape = ksel.shape
    lane = lax.broadcasted_iota(jnp.int32, shape, 1)
    row = lax.broadcasted_iota(jnp.int32, shape, 0)
    lo = lane < HALF
    pos = (pl.program_id(0) % tiles_per_seq) * TM + row
    onehot = jnp.where(lane == jnp.right_shift(pos, 6), 1.0, 0.0).astype(BF16)
    ksel_sw = pltpu.roll(ksel, HALF, 1)
    kwin_sw = pltpu.roll(kwin, HALF, 1)
    for g, (ks_lo, ks_hi, kw_lo, kw_hi) in enumerate(((ksel, ksel_sw, kwin, kwin_sw),
                                                      (ksel_sw, ksel, kwin_sw, kwin))):
        ksa_ref[g] = jnp.concatenate([jnp.where(lo, ks_lo, 0.0).astype(BF16), onehot], axis=1)
        ksb_ref[g] = jnp.concatenate([jnp.where(lo, 0.0, ks_hi).astype(BF16), onehot], axis=1)
        kwa_ref[g] = jnp.where(lo, kw_lo, 0.0).astype(BF16)
        kwb_ref[g] = jnp.where(lo, 0.0, kw_hi).astype(BF16)

    vsel_t = p[:, _OFF_VS:_OFF_VS + KV_W].T
    vwin_t = p[:, _OFF_VW:_OFF_VW + KV_W].T
    ones_row = jnp.where(lax.broadcasted_iota(jnp.int32, (V_ROWS - HEAD_DIM, TM), 0) == 0, 1.0, 0.0)
    for g in range(N_KV_GROUPS):
        rows = slice(g * HEAD_DIM, (g + 1) * HEAD_DIM)
        vs_aug = jnp.concatenate([vsel_t[rows, :], ones_row], axis=0).astype(BF16)
        vw_aug = jnp.concatenate([vwin_t[rows, :], ones_row], axis=0).astype(BF16)
        vst_ref[g, 0] = vs_aug
        for r in range(TM // TQ):
            vsd_ref[g, r] = vs_aug[:, r * TQ:(r + 1) * TQ]
            vwt_ref[g, r] = vw_aug[:, r * TQ:(r + 1) * TQ]

    gn_ref[...] = jax.nn.sigmoid(p[:, _OFF_GN:_OFF_GN + 2 * LANES])

    ge = _gelu_tanh(p[:, _OFF_UV:_OFF_UV + 2 * SGU_W])
    u = ge[:, :SGU_W]
    v = _rms(ge[:, SGU_W:], sgn_ref[...])
    tri = (lax.broadcasted_iota(jnp.int32, (CHUNK, CHUNK), 0)
           >= lax.broadcasted_iota(jnp.int32, (CHUNK, CHUNK), 1))
    lo_c = lax.broadcasted_iota(jnp.int32, (CHUNK, LANES), 1) < HALF
    n_chunks = TM // CHUNK
    cols = []
    for pr in range(SGU_W // LANES):
        w_even = jnp.where(tri, ws_ref[2 * pr], 0.0).astype(BF16)
        w_odd = jnp.where(tri, ws_ref[2 * pr + 1], 0.0).astype(BF16)
        rows = []
        for r in range(n_chunks):
            vp = v[r * CHUNK:(r + 1) * CHUNK, pr * LANES:(pr + 1) * LANES]
            v_even = jnp.where(lo_c, vp, 0.0).astype(BF16)
            v_odd = jnp.where(lo_c, 0.0, vp).astype(BF16)
            rows.append(_dot(w_even, v_even) + _dot(w_odd, v_odd))
        cols.append(jnp.concatenate(rows, axis=0))
    vo = jnp.concatenate(cols, axis=1)
    bias = jnp.concatenate([bs_ref[...]] * n_chunks, axis=0)
    yb_ref[...] = (u * (vo + bias)).astype(BF16)


def _inproj(x2, g, w, cos, sin, sgn, ws, bs, seq):
    t, d = x2.shape
    assert TM == TK
    tiles_per_seq = seq // TM
    const2 = lambda i: (0, 0)
    row = lambda i: (i, 0)
    grp = lambda i: (0, i, 0)
    grp4 = lambda i: (0, i, 0, 0)
    pos = lambda i: (i % tiles_per_seq, 0)
    ng = N_KV_GROUPS
    return pl.pallas_call(
        functools.partial(_inproj_body, tiles_per_seq=tiles_per_seq),
        grid=(t // TM,),
        in_specs=[
            pl.BlockSpec((TM, d), row),
            pl.BlockSpec((1, d), const2),
            pl.BlockSpec((d, _IN_COLS), const2, pipeline_mode=pl.Buffered(1)),
            pl.BlockSpec((TM, LANES), pos),
            pl.BlockSpec((TM, LANES), pos),
            pl.BlockSpec((1, SGU_W), const2),
            pl.BlockSpec((SGU_GROUPS, CHUNK, CHUNK), lambda i: (0, 0, 0)),
            pl.BlockSpec((CHUNK, SGU_W), const2),
        ],
        out_specs=[
            pl.BlockSpec((TM, Q_W), row),
            pl.BlockSpec((TM, KV_W), row),
            pl.BlockSpec((TM, KV_W), row),
            pl.BlockSpec((ng, TM, 2 * LANES), grp),
            pl.BlockSpec((ng, TM, 2 * LANES), grp),
            pl.BlockSpec((ng, 1, V_ROWS, TK), grp4),
            pl.BlockSpec((ng, TM // TQ, V_ROWS, TQ), grp4),
            pl.BlockSpec((ng, TM, LANES), grp),
            pl.BlockSpec((ng, TM, LANES), grp),
            pl.BlockSpec((ng, TM // TQ, V_ROWS, TQ), grp4),
            pl.BlockSpec((TM, 2 * LANES), row),
            pl.BlockSpec((TM, SGU_W), row),
        ],
        out_shape=[
            jax.ShapeDtypeStruct((t, Q_W), BF16),
            jax.ShapeDtypeStruct((t, KV_W), F32),
            jax.ShapeDtypeStruct((t, KV_W), F32),
            jax.ShapeDtypeStruct((ng, t, 2 * LANES), BF16),
            jax.ShapeDtypeStruct((ng, t, 2 * LANES), BF16),
            jax.ShapeDtypeStruct((ng, t // TK, V_ROWS, TK), BF16),
            jax.ShapeDtypeStruct((ng, t // TQ, V_ROWS, TQ), BF16),
            jax.ShapeDtypeStruct((ng, t, LANES), BF16),
            jax.ShapeDtypeStruct((ng, t, LANES), BF16),
            jax.ShapeDtypeStruct((ng, t // TQ, V_ROWS, TQ), BF16),
            jax.ShapeDtypeStruct((t, 2 * LANES), F32),
            jax.ShapeDtypeStruct((t, SGU_W), BF16),
        ],
        compiler_params=_params("parallel"),
        name="inproj",
    )(x2, g, w, cos, sin, sgn, ws, bs)


def _compress_body(kc_ref, vc_ref, pekt_ref, pekb_ref, pevt_ref, pevb_ref,
                   wkt_ref, wkb_ref, wvt_ref, wvb_ref, w2k_ref, w2vt_ref, cos_ref, sin_ref,
                   kca_ref, kcb_ref, vct_ref):
    n_rows = kc_ref.shape[0]

    def hidden(c_ref, pet_ref, peb_ref, wt_ref, wb_ref):
        c = c_ref[...]
        top = _dot((c + pet_ref[...]).astype(BF16), wt_ref[...])
        bot = _dot((c + peb_ref[...]).astype(BF16), wb_ref[...])
        return _gelu_tanh(top + pltpu.roll(bot, n_rows - 1, 0))

    hk = hidden(kc_ref, pekt_ref, pekb_ref, wkt_ref, wkb_ref)
    hv = hidden(vc_ref, pevt_ref, pevb_ref, wvt_ref, wvb_ref)
    cos = cos_ref[...]
    sin = sin_ref[...]
    for g in range(N_KV_GROUPS):
        hkg = hk[:, g * PHI_HIDDEN:(g + 1) * PHI_HIDDEN].astype(BF16)
        hvg = hv[:, g * PHI_HIDDEN:(g + 1) * PHI_HIDDEN].astype(BF16)
        kk = _dot(hkg, w2k_ref[...])
        kca_ref[g] = (kk[:, 0:LANES] * cos + kk[:, 2 * LANES:3 * LANES] * sin).astype(BF16)
        kcb_ref[g] = (kk[:, LANES:2 * LANES] * cos + kk[:, 3 * LANES:4 * LANES] * sin).astype(BF16)
        vct_ref[g, 0] = _dot_nt(w2vt_ref[...], hvg).astype(BF16)


def _compress(kc_rows, vc_rows, pe, w1, w2k, w2vt, cos_c, sin_c, batch):
    total, width = kc_rows.shape
    n_rows = total // batch
    const = lambda b: (0, 0)
    k_shape = jax.ShapeDtypeStruct((N_KV_GROUPS, total, LANES), BF16)
    k_spec = pl.BlockSpec((N_KV_GROUPS, n_rows, LANES), lambda b: (0, b, 0))
    pe_spec = pl.BlockSpec((1, width), const)
    w1_spec = pl.BlockSpec((width, N_KV_GROUPS * PHI_HIDDEN), const)
    return pl.pallas_call(
        _compress_body,
        grid=(batch,),
        in_specs=[
            pl.BlockSpec((n_rows, width), lambda b: (b, 0)),
            pl.BlockSpec((n_rows, width), lambda b: (b, 0)),
            pe_spec, pe_spec, pe_spec, pe_spec,
            w1_spec, w1_spec, w1_spec, w1_spec,
            pl.BlockSpec((PHI_HIDDEN, 4 * LANES), const),
            pl.BlockSpec((HEAD_DIM, PHI_HIDDEN), const),
            pl.BlockSpec((n_rows, LANES), const),
            pl.BlockSpec((n_rows, LANES), const),
        ],
        out_specs=[k_spec, k_spec,
                   pl.BlockSpec((N_KV_GROUPS, 1, HEAD_DIM, n_rows), lambda b: (0, b, 0, 0))],
        out_shape=[k_shape, k_shape,
                   jax.ShapeDtypeStruct((N_KV_GROUPS, batch, HEAD_DIM, n_rows), BF16)],
        compiler_params=_params("parallel"),
        name="compress",
    )(kc_rows, vc_rows, *pe, *w1, w2k, w2vt, cos_c, sin_c)


CMP_CLASSES = 4


def _cmp_body(q_ref, kca_ref, kcb_ref, vct_ref, gn_ref, selt_ref, o_ref, bias_ref):
    n_cmp = kca_ref.shape[1]
    n_blk = selt_ref.shape[0]
    n_col = HEADS_PER_GROUP * TQC
    qi = pl.program_id(2)
    t0 = qi * TQC
    lq = jnp.concatenate([q_ref[:, pr * LANES:(pr + 1) * LANES] for pr in range(HEAD_PAIRS)], axis=0)
    tok = t0 + jnp.bitwise_and(lax.broadcasted_iota(jnp.int32, (1, n_col), 1), TQC - 1)
    any_valid = jnp.where(tok >= L_CMP - 1, 1.0, 0.0)

    def run(rows_cmp, rows_blk):
        s = jnp.concatenate([_dot_nt(kca_ref[0, 0:rows_cmp, :], lq),
                             _dot_nt(kcb_ref[0, 0:rows_cmp, :], lq)], axis=1)
        cmp_end = lax.broadcasted_iota(jnp.int32, (rows_cmp, 1), 0) * CMP_STRIDE + (L_CMP - 1)
        s = jnp.where(cmp_end <= tok, s, NEG_INF)
        e = jnp.exp2(s - jnp.max(s, axis=0, keepdims=True))
        p = e * (any_valid / jnp.sum(e, axis=0, keepdims=True))
        oc_t = _dot(vct_ref[0, 0, :, 0:rows_cmp], p.astype(BF16))
        psum = p[:, 0:TQC]
        for hh in range(1, HEADS_PER_GROUP):
            psum = psum + p[:, hh * TQC:(hh + 1) * TQC]

        selt = selt_ref[0:rows_blk, 0:rows_cmp]
        p_hi = psum.astype(BF16)
        r1 = psum - p_hi.astype(F32)
        p_mid = r1.astype(BF16)
        p_lo = (r1 - p_mid.astype(F32)).astype(BF16)
        imp = _dot(selt, p_hi) + _dot(selt, p_mid) + _dot(selt, p_lo)

        jj = lax.broadcasted_iota(jnp.int32, (rows_blk, TQC), 0)
        cur = jnp.right_shift(t0 + lax.broadcasted_iota(jnp.int32, (rows_blk, TQC), 1), 6)
        jf = jj.astype(F32)
        taken = -3e38
        val = jnp.where(jj < cur, imp, -FORCE)
        for _ in range(N_TOPK - 1):
            best = jnp.max(val, axis=0, keepdims=True)
            first = jnp.min(jnp.where(val == best, jf, 1e6), axis=0, keepdims=True)
            val = jnp.where(jf == first, taken, val)
        bias_t = jnp.where(jnp.logical_and(val == taken, jj < cur), 0.0, NEG_INF)
        if rows_blk < n_blk:
            bias_t = jnp.concatenate([bias_t, jnp.full((n_blk - rows_blk, TQC), NEG_INF, F32)], axis=0)

        for u in range(TQC // LANES):
            rows = slice(u * LANES, (u + 1) * LANES)
            gn_t = gn_ref[rows, :].T
            bias_ref[0, rows, :] = bias_t[:, rows].T.astype(BF16)
            for pr in range(HEAD_PAIRS):
                c_even = pr * TQC + u * LANES
                c_odd = (HEAD_PAIRS + pr) * TQC + u * LANES
                even = oc_t[:, c_even:c_even + LANES] * gn_t[2 * pr:2 * pr + 1, :]
                odd = oc_t[:, c_odd:c_odd + LANES] * gn_t[2 * pr + 1:2 * pr + 2, :]
                o_ref[rows, pr * LANES:(pr + 1) * LANES] = jnp.concatenate([even, odd], axis=0).T

    tiles_per_class = pl.num_programs(2) // CMP_CLASSES
    cls = qi // tiles_per_class
    for c in range(CMP_CLASSES):
        pl.when(cls == c)(functools.partial(run, n_cmp * (c + 1) // CMP_CLASSES, n_blk * (c + 1) // CMP_CLASSES))


def _cmp_attention(q, kca, kcb, vct, gn, selt, batch, seq):
    t = q.shape[0]
    n_cmp = kca.shape[1] // batch
    nq = seq // TQC
    k_spec = pl.BlockSpec((1, n_cmp, LANES), lambda b, g, i: (g, b, 0))
    return pl.pallas_call(
        _cmp_body,
        grid=(batch, N_KV_GROUPS, nq),
        in_specs=[
            pl.BlockSpec((TQC, 2 * LANES), lambda b, g, i: (b * nq + i, g)),
            k_spec, k_spec,
            pl.BlockSpec((1, 1, HEAD_DIM, n_cmp), lambda b, g, i: (g, b, 0, 0)),
            pl.BlockSpec((TQC, LANES), lambda b, g, i: (b * nq + i, g)),
            pl.BlockSpec(selt.shape, lambda b, g, i: (0, 0)),
        ],
        out_specs=[
            pl.BlockSpec((TQC, 2 * LANES), lambda b, g, i: (b * nq + i, g)),
            pl.BlockSpec((1, TQC, LANES), lambda b, g, i: (g, b * nq + i, 0)),
        ],
        out_shape=[
            jax.ShapeDtypeStruct((t, Q_W), F32),
            jax.ShapeDtypeStruct((N_KV_GROUPS, t, LANES), BF16),
        ],
        compiler_params=_params("parallel", "parallel", "parallel"),
        name="cmp_attn",
    )(q, kca, kcb, vct, gn, selt)


def _selwin_body(q_ref, bias_ref, ksa_ref, ksb_ref, vst_ref, vsd_ref, kwa_ref, kwb_ref, vwt_ref, gn_ref,
                 oc_ref, tri_ref, o_ref, *scratch):
    per_set = len(scratch) // 2
    for k in range(SELWIN_TILES):
        _selwin_tile(k, q_ref, bias_ref, ksa_ref, ksb_ref, vst_ref, vsd_ref, kwa_ref, kwb_ref, vwt_ref, gn_ref,
                     oc_ref, tri_ref, o_ref, *scratch[(k % 2) * per_set:(k % 2 + 1) * per_set])


def _selwin_tile(k, q_ref, bias_ref, ksa_ref, ksb_ref, vst_ref, vsd_ref, kwa_ref, kwb_ref, vwt_ref, gn_ref,
                 oc_ref, tri_ref, o_ref, acc_ref, sa_ref, sb_ref, pa_ref, pb_ref, sw_ref):
    qi = pl.program_id(2) * SELWIN_TILES + k
    rows = slice(k * TQ, (k + 1) * TQ)
    t0 = qi * TQ
    n_col = HEADS_PER_GROUP * TQ
    q_pairs = [q_ref[rows, pr * LANES:(pr + 1) * LANES] for pr in range(HEAD_PAIRS)]
    lq = jnp.concatenate(q_pairs, axis=0)
    bias = bias_ref[0, rows, :]
    ls = jnp.concatenate([jnp.concatenate([qp, bias], axis=1) for qp in q_pairs], axis=0)
    tok = t0 + jnp.bitwise_and(lax.broadcasted_iota(jnp.int32, (1, n_col), 1), TQ - 1)
    jd = t0 // TK

    def scores(j):
        k_lo = pl.multiple_of(j * TK, TK)
        s = jnp.concatenate([_dot_nt(ksa_ref[0, pl.ds(k_lo, TK), :], ls),
                             _dot_nt(ksb_ref[0, pl.ds(k_lo, TK), :], ls)], axis=1)
        return s, jnp.max(s, axis=0, keepdims=True)

    def weights(s, mx, m_old):
        m_new = jnp.maximum(m_old, mx)
        return m_new, jnp.exp2(m_old - m_new), jnp.exp2(s - m_new).astype(BF16)

    neg_inf = jnp.full((1, n_col), -jnp.inf, F32)
    acc_ref[...] = jnp.zeros(acc_ref.shape, F32)
    pb_ref[...] = jnp.zeros(pb_ref.shape, BF16)
    sa_ref[...], mx0 = scores(0)
    sb_ref[...], mx1 = scores(1)

    n_wt = W_WIN // TQ + 1
    win_tiles = []
    parts = []
    for r in range(n_wt):
        kt = qi - (n_wt - 1) + r
        kt_c = jnp.maximum(kt, 0)
        k_r = pl.multiple_of(kt_c * TQ, TQ)
        s_r = jnp.concatenate([_dot_nt(kwa_ref[0, pl.ds(k_r, TQ), :], lq),
                               _dot_nt(kwb_ref[0, pl.ds(k_r, TQ), :], lq)], axis=1)
        if r == 0:
            s_r = s_r + tri_ref[0]
        if r == n_wt - 1:
            s_r = s_r + tri_ref[1]
        else:
            s_r = jnp.where(kt >= 0, s_r, NEG_INF)
        parts.append(s_r)
        win_tiles.append(kt_c)
    sw = jnp.concatenate(parts, axis=0)
    mx_w = jnp.max(sw, axis=0, keepdims=True)
    sw_ref[...] = sw

    def pair(j, carry):
        m, mx_a, mx_b, a_prev = carry
        m, a_cur, pa_ref[...] = weights(sa_ref[...], mx_a, m)
        sa_ref[...], mx_a = scores(j + 2)
        acc_ref[...] = a_prev * acc_ref[...] + _dot(vst_ref[0, jnp.maximum(j - 1, 0)], pb_ref[...])
        m, a_next, pb_ref[...] = weights(sb_ref[...], mx_b, m)
        sb_ref[...], mx_b = scores(j + 3)
        acc_ref[...] = a_cur * acc_ref[...] + _dot(vst_ref[0, j], pa_ref[...])
        return m, mx_a, mx_b, a_next

    carry = (neg_inf, mx0, mx1, jnp.ones((1, n_col), F32))
    m, mx_a, mx_b, a_prev = lax.fori_loop(0, jd // 2, lambda i, c: pair(2 * i, c), carry)

    jf = 2 * (jd // 2)
    m, a_cur, pa_ref[...] = weights(sa_ref[...], mx_a, m)
    acc_ref[...] = a_prev * acc_ref[...] + _dot(vst_ref[0, jnp.maximum(jf - 1, 0)], pb_ref[...])
    m, a_next, pb_ref[...] = weights(sb_ref[...], mx_b, m)
    acc_ref[...] = a_cur * acc_ref[...] + _dot(vst_ref[0, jf], pa_ref[...])
    kd = pl.multiple_of(t0, TQ)
    sd = jnp.concatenate([_dot_nt(ksa_ref[0, pl.ds(kd, TQ), 0:LANES], lq),
                          _dot_nt(ksb_ref[0, pl.ds(kd, TQ), 0:LANES], lq)], axis=1)
    key_d = t0 + lax.broadcasted_iota(jnp.int32, (TQ, 1), 0)
    own = jnp.logical_and(jnp.right_shift(key_d, 6) == jnp.right_shift(tok, 6), key_d <= tok)
    sd = jnp.where(own, sd, NEG_INF)
    _, a_own, pd = weights(sd, jnp.max(sd, axis=0, keepdims=True), m)
    o_sel = a_next * acc_ref[...] + _dot(vst_ref[0, jf + 1], pb_ref[...])
    o_sel = a_own * o_sel + _dot(vsd_ref[0, qi], pd)
    pw = jnp.exp2(sw_ref[...] - mx_w).astype(BF16)
    o_win = _dot(vwt_ref[0, win_tiles[0]], pw[0:TQ])
    for r in range(1, n_wt):
        o_win = o_win + _dot(vwt_ref[0, win_tiles[r]], pw[r * TQ:(r + 1) * TQ])
    os_t = o_sel[:HEAD_DIM] * (1.0 / o_sel[HEAD_DIM:HEAD_DIM + 1])
    ow_t = o_win[:HEAD_DIM] * (1.0 / o_win[HEAD_DIM:HEAD_DIM + 1])

    gn_t = gn_ref[rows, :].T
    for pr in range(HEAD_PAIRS):
        halves = []
        for e in range(2):
            c0 = (e * HEAD_PAIRS + pr) * TQ
            hh = 2 * pr + e
            g_sel = gn_t[HEADS_PER_GROUP + hh:HEADS_PER_GROUP + hh + 1, :]
            g_win = gn_t[2 * HEADS_PER_GROUP + hh:2 * HEADS_PER_GROUP + hh + 1, :]
            halves.append(os_t[:, c0:c0 + TQ] * g_sel + ow_t[:, c0:c0 + TQ] * g_win)
        cols = slice(pr * LANES, (pr + 1) * LANES)
        o_ref[rows, cols] = (jnp.concatenate(halves, axis=0).T + oc_ref[rows, cols]).astype(BF16)


def _selwin_attention(q, bias, ksa, ksb, vst, vsd, kwa, kwb, vwt, gn, oc, tri, batch, seq):
    t = q.shape[0]
    rows = SELWIN_TILES * TQ
    nq = seq // rows
    n_col = HEADS_PER_GROUP * TQ
    tile_row = lambda b, g, i: (b * nq + i, g)
    grp = lambda b, g, i: (g, b, 0)
    grp4 = lambda b, g, i: (g, b, 0, 0)
    once = pl.Buffered(1)
    scratch_set = [pltpu.VMEM((V_ROWS, n_col), F32),
                   pltpu.VMEM((TK, n_col), F32), pltpu.VMEM((TK, n_col), F32),
                   pltpu.VMEM((TK, n_col), BF16), pltpu.VMEM((TK, n_col), BF16),
                   pltpu.VMEM((W_WIN + TQ, n_col), F32)]
    return pl.pallas_call(
        _selwin_body,
        grid=(batch, N_KV_GROUPS, nq),
        in_specs=[
            pl.BlockSpec((rows, 2 * LANES), tile_row),
            pl.BlockSpec((1, rows, LANES), lambda b, g, i: (g, b * nq + i, 0)),
            pl.BlockSpec((1, seq, 2 * LANES), grp, pipeline_mode=once),
            pl.BlockSpec((1, seq, 2 * LANES), grp, pipeline_mode=once),
            pl.BlockSpec((1, seq // TK, V_ROWS, TK), grp4, pipeline_mode=once),
            pl.BlockSpec((1, seq // TQ, V_ROWS, TQ), grp4, pipeline_mode=once),
            pl.BlockSpec((1, seq, LANES), grp, pipeline_mode=once),
            pl.BlockSpec((1, seq, LANES), grp, pipeline_mode=once),
            pl.BlockSpec((1, seq // TQ, V_ROWS, TQ), grp4, pipeline_mode=once),
            pl.BlockSpec((rows, LANES), tile_row),
            pl.BlockSpec((rows, 2 * LANES), tile_row),
            pl.BlockSpec(tri.shape, lambda b, g, i: (0, 0, 0), pipeline_mode=once),
        ],
        out_specs=pl.BlockSpec((rows, 2 * LANES), tile_row),
        out_shape=jax.ShapeDtypeStruct((t, Q_W), BF16),
        scratch_shapes=scratch_set + scratch_set,
        compiler_params=_params("parallel", "parallel", "arbitrary"),
        name="selwin_attn",
    )(q, bias, ksa, ksb, vst, vsd, kwa, kwb, vwt, gn, oc, tri)


def _merge_body(x_ref, g_ref, wg_ref, ya_ref, yb_ref, pa_ref, pb_ref, wo_ref, o_ref):
    x = x_ref[...]
    d = x.shape[1]
    h = _rms(x, g_ref[...]).astype(BF16)
    gates = jax.nn.sigmoid(_dot(h, wg_ref[...]))
    merged = gates[:, :d] * _dot(ya_ref[...], pa_ref[...]) + gates[:, d:] * _dot(yb_ref[...], pb_ref[...])
    o_ref[...] = x + _dot(merged.astype(BF16), wo_ref[...])


def _merge(x2, g, wg, ya, yb, pa, pb, wo):
    t, d = x2.shape
    const = lambda i: (0, 0)
    row = lambda i: (i, 0)
    return pl.pallas_call(
        _merge_body,
        grid=(t // TM,),
        in_specs=[
            pl.BlockSpec((TM, d), row),
            pl.BlockSpec((1, d), const),
            pl.BlockSpec((d, 2 * d), const),
            pl.BlockSpec((TM, Q_W), row),
            pl.BlockSpec((TM, SGU_W), row),
            pl.BlockSpec((Q_W, d), const),
            pl.BlockSpec((SGU_W, d), const),
            pl.BlockSpec((d, d), const),
        ],
        out_specs=pl.BlockSpec((TM, d), row),
        out_shape=jax.ShapeDtypeStruct((t, d), F32),
        compiler_params=_params("parallel"),
        name="merge",
    )(x2, g, wg, ya, yb, pa, pb, wo)


def _rot_cols(w, heads):
    d = w.shape[0]
    w4 = w.reshape(d, heads, 2, HEAD_DIM // 2)
    return jnp.concatenate([-w4[:, :, 1], w4[:, :, 0]], axis=-1).reshape(d, heads * HEAD_DIM)


def _rope_tables(pos):
    inv = ROPE_THETA ** (-jnp.arange(0, HEAD_DIM, 2, dtype=F32) / HEAD_DIM)
    ang = pos.astype(F32)[:, None] * inv[None, :]
    cos = jnp.concatenate([jnp.cos(ang)] * (2 * LANES // HEAD_DIM), axis=-1)
    sin = jnp.concatenate([jnp.sin(ang)] * (2 * LANES // HEAD_DIM), axis=-1)
    return cos, sin


def _inproj_weight(w_in):
    q = w_in[:, 0:Q_W]
    ks = w_in[:, _OFF_KS:_OFF_KS + KV_W]
    kw = w_in[:, _OFF_KW:_OFF_KW + KV_W]
    gn0 = Q_W + 6 * KV_W
    zero = jnp.zeros((w_in.shape[0], LANES - 3 * HEADS_PER_GROUP), w_in.dtype)
    gn = []
    for g in range(N_KV_GROUPS):
        w_g = w_in[:, gn0 + g * 3 * HEADS_PER_GROUP:gn0 + (g + 1) * 3 * HEADS_PER_GROUP]
        gn += [w_g.reshape(-1, HEADS_PER_GROUP, 3).transpose(0, 2, 1).reshape(-1, 3 * HEADS_PER_GROUP), zero]
    uv0 = gn0 + NSA_GATE_W
    uv = w_in[:, uv0:uv0 + 2 * SGU_W]
    w = jnp.concatenate([w_in[:, :gn0], _rot_cols(q, N_HEADS), _rot_cols(ks, N_KV_GROUPS),
                         _rot_cols(kw, N_KV_GROUPS), *gn, uv], axis=1)
    assert w.shape[1] == _IN_COLS
    return w.astype(BF16), w_in[:, uv0 + 2 * SGU_W:].astype(BF16)


def _compress_params(pe_k, pe_v, w1k, w1v, w2k, w2v):
    half = L_CMP // 2

    def pe_rows(pe):
        rep = jnp.broadcast_to(pe[:, None, :], (L_CMP, N_KV_GROUPS, HEAD_DIM))
        return (rep[:half].reshape(1, -1), rep[half:].reshape(1, -1))

    def w1_halves(w1):
        w = w1.reshape(L_CMP, HEAD_DIM, PHI_HIDDEN)
        zero = jnp.zeros((half, HEAD_DIM, PHI_HIDDEN), w.dtype)
        outs = []
        for part in (w[:half], w[half:]):
            blocks = jnp.stack([jnp.concatenate([part, zero], axis=-1),
                                jnp.concatenate([zero, part], axis=-1)], axis=1)
            outs.append(blocks.reshape(half * N_KV_GROUPS * HEAD_DIM, N_KV_GROUPS * PHI_HIDDEN).astype(BF16))
        return outs

    zero = jnp.zeros_like(w2k)
    w2k_rot = _rot_cols(w2k, 1)
    w2k_all = jnp.concatenate([w2k, zero, zero, w2k, w2k_rot, zero, zero, w2k_rot], axis=1).astype(BF16)
    pkt, pkb = pe_rows(pe_k)
    pvt, pvb = pe_rows(pe_v)
    return (pkt, pkb, pvt, pvb), (*w1_halves(w1k), *w1_halves(w1v)), w2k_all, w2v.T.astype(BF16)


def _window_triangles():
    key = np.arange(TQ)[:, None]
    tok = np.arange(HEADS_PER_GROUP * TQ)[None, :] % TQ
    upper = np.where(key > tok, 0.0, NEG_INF)
    lower = np.where(key <= tok, 0.0, NEG_INF)
    return jnp.asarray(np.stack([upper, lower]), dtype=F32)


def _sel_map_t(seq):
    n_rows = seq // CMP_STRIDE
    n_cmp = (seq - L_CMP) // CMP_STRIDE + 1
    cmp_start = np.arange(n_rows) * CMP_STRIDE
    sel_start = np.arange(LANES) * L_SEL
    ov = (np.minimum(cmp_start[None, :] + L_CMP, sel_start[:, None] + L_SEL)
          - np.maximum(cmp_start[None, :], sel_start[:, None]))
    m = np.clip(ov, 0, None).astype(np.float32) / L_CMP
    m[:, n_cmp:] = 0.0
    m[seq // L_SEL:, :] = 0.0
    return jnp.asarray(m, dtype=BF16)


def kernel(x, ffn1_norm, ffn1_w_gate_up, ffn1_w_down, mix_norm, w_in, cmp_pos_k, cmp_pos_v,
           phi_k_w1, phi_k_w2, phi_v_w1, phi_v_w2, sgu_norm, sgu_w_s, sgu_b_s, proj_a, proj_b,
           w_out, ffn2_norm, ffn2_w_gate_up, ffn2_w_down, final_norm):
    batch, seq, d = x.shape
    depth = w_in.shape[0]
    assert seq % TM == 0 and seq % TK == 0 and seq >= N_TOPK * L_SEL and seq // L_SEL <= LANES
    assert seq % (2 * TK) == 0 and seq >= W_WIN + TQ and W_WIN % TQ == 0 and seq % (SELWIN_TILES * TQ) == 0 and N_KV_GROUPS == 2 and HEADS_PER_GROUP == 4
    t = batch * seq
    n_rows = seq // CMP_STRIDE

    cos, sin = _rope_tables(jnp.arange(seq))
    cos_c, sin_c = _rope_tables(jnp.arange(n_rows) * CMP_STRIDE + (L_CMP - 1))
    selt = _sel_map_t(seq)
    tri = _window_triangles()
    row = lambda v: v.reshape(1, -1)

    x2 = x.reshape(t, d)
    for l in range(depth):
        x2 = _ffn(x2, row(ffn1_norm[l]), ffn1_w_gate_up[l].astype(BF16), ffn1_w_down[l].astype(BF16),
                  row(final_norm), False)

        w_main, w_gates = _inproj_weight(w_in[l])
        bs = jnp.repeat(sgu_b_s[l].T, SGU_HEAD, axis=1)
        (q, kc, vc, ksa, ksb, vst, vsd, kwa, kwb, vwt, gn, yb) = _inproj(
            x2, row(mix_norm[l]), w_main, cos, sin, row(sgu_norm[l]), sgu_w_s[l], bs, seq)

        pe, w1, w2k_all, w2vt = _compress_params(
            cmp_pos_k[l], cmp_pos_v[l], phi_k_w1[l], phi_v_w1[l], phi_k_w2[l], phi_v_w2[l])
        width = CMP_STRIDE * KV_W
        kca, kcb, vct = _compress(kc.reshape(t // CMP_STRIDE, width), vc.reshape(t // CMP_STRIDE, width),
                                  pe, w1, w2k_all, w2vt, cos_c, sin_c, batch)

        o_cmp, bias = _cmp_attention(q, kca, kcb, vct, gn, selt, batch, seq)
        ya = _selwin_attention(q, bias, ksa, ksb, vst, vsd, kwa, kwb, vwt, gn, o_cmp, tri, batch, seq)

        x2 = _merge(x2, row(mix_norm[l]), w_gates, ya, yb,
                    proj_a[l].astype(BF16), proj_b[l].astype(BF16), w_out[l].astype(BF16))

        x2 = _ffn(x2, row(ffn2_norm[l]), ffn2_w_gate_up[l].astype(BF16), ffn2_w_down[l].astype(BF16),
                  row(final_norm), l == depth - 1)
    return x2.reshape(batch, seq, d)
```

```python
import functools

import jax
import jax.numpy as jnp
import numpy as np
from jax import lax
from jax.experimental import pallas as pl
from jax.experimental.pallas import tpu as pltpu

HEAD_DIM = 64
N_HEADS = 8
N_KV_GROUPS = 2
HEADS_PER_GROUP = N_HEADS // N_KV_GROUPS
HEAD_PAIRS = HEADS_PER_GROUP // 2
Q_W = N_HEADS * HEAD_DIM
KV_W = N_KV_GROUPS * HEAD_DIM
NSA_GATE_W = N_HEADS * 3
L_CMP = 32
CMP_STRIDE = 16
L_SEL = 64
N_TOPK = 16
W_WIN = 512
PHI_HIDDEN = 256
SGU_GROUPS = 8
SGU_HEAD = 64
SGU_W = SGU_GROUPS * SGU_HEAD
CHUNK = 128
ROPE_THETA = 10000.0
EPS = 1e-6
NEG_INF = -1e30
FORCE = 1e9
SCALE = HEAD_DIM ** -0.5
LOG2E = float(np.log2(np.e))
V_ROWS = HEAD_DIM + 16

LANES = 128
HALF = LANES // 2
VMEM_LIMIT = 56 * 1024 * 1024

TM = 512
TQ = 256
TQC = 256
TK = 512
SELWIN_TILES = 4
FF_CHUNK = 256

F32 = jnp.float32
BF16 = jnp.bfloat16


def _dot(a, b):
    return jnp.dot(a, b, preferred_element_type=F32)


def _dot_nt(a, b):
    return lax.dot_general(a, b, (((1,), (1,)), ((), ())), preferred_element_type=F32)


def _rms(x, g):
    return x * lax.rsqrt(jnp.mean(x * x, axis=-1, keepdims=True) + EPS) * g


def _gelu_tanh(x):
    c = np.float32(np.sqrt(2.0 / np.pi))
    return 0.5 * x * (1.0 + jnp.tanh(c * (x + 0.044715 * (x * x * x))))


def _params(*sem):
    return pltpu.CompilerParams(dimension_semantics=sem, vmem_limit_bytes=VMEM_LIMIT)


def _ffn_body(x_ref, g_ref, wgu_ref, wd_ref, fg_ref, o_ref, *, d_ff, final_norm):
    x = x_ref[...]
    h = _rms(x, g_ref[...]).astype(BF16)
    acc = jnp.zeros(x.shape, F32)
    for c in range(d_ff // FF_CHUNK):
        lo = c * FF_CHUNK
        gate = _dot(h, wgu_ref[:, lo:lo + FF_CHUNK])
        up = _dot(h, wgu_ref[:, d_ff + lo:d_ff + lo + FF_CHUNK])
        act = (gate * jax.nn.sigmoid(gate) * up).astype(BF16)
        acc = acc + _dot(act, wd_ref[lo:lo + FF_CHUNK, :])
    y = x + 0.5 * acc
    if final_norm:
        y = _rms(y, fg_ref[...])
    o_ref[...] = y


def _ffn(x2, g, wgu, wd, fg, final_norm):
    t, d = x2.shape
    d_ff = wd.shape[0]
    assert t % TM == 0 and d_ff % FF_CHUNK == 0
    const = lambda i: (0, 0)
    return pl.pallas_call(
        functools.partial(_ffn_body, d_ff=d_ff, final_norm=final_norm),
        grid=(t // TM,),
        in_specs=[
            pl.BlockSpec((TM, d), lambda i: (i, 0)),
            pl.BlockSpec((1, d), const),
            pl.BlockSpec((d, 2 * d_ff), const, pipeline_mode=pl.Buffered(1)),
            pl.BlockSpec((d_ff, d), const, pipeline_mode=pl.Buffered(1)),
            pl.BlockSpec((1, d), const),
        ],
        out_specs=pl.BlockSpec((TM, d), lambda i: (i, 0)),
        out_shape=jax.ShapeDtypeStruct((t, d), F32),
        compiler_params=_params("parallel"),
        name="ffn",
    )(x2, g, wgu, wd, fg)


_OFF_Q, _OFF_KC, _OFF_VC, _OFF_KS, _OFF_VS, _OFF_KW, _OFF_VW = 0, 512, 640, 768, 896, 1024, 1152
_OFF_RQ, _OFF_RKS, _OFF_RKW, _OFF_GN, _OFF_UV = 1280, 1792, 1920, 2048, 2304
_IN_COLS = _OFF_UV + 2 * SGU_W


def _inproj_body(x_ref, g_ref, w_ref, cos_ref, sin_ref, sgn_ref, ws_ref, bs_ref,
                 q_ref, kc_ref, vc_ref, ksa_ref, ksb_ref, vst_ref, vsd_ref, kwa_ref, kwb_ref, vwt_ref,
                 gn_ref, yb_ref, *, tiles_per_seq):
    h = _rms(x_ref[...], g_ref[...]).astype(BF16)
    p = _dot(h, w_ref[...])
    cos = cos_ref[...]
    sin = sin_ref[...]
    cos4 = jnp.concatenate([cos] * (Q_W // LANES), axis=1)
    sin4 = jnp.concatenate([sin] * (Q_W // LANES), axis=1)

    q = p[:, _OFF_Q:_OFF_Q + Q_W] * cos4 + p[:, _OFF_RQ:_OFF_RQ + Q_W] * sin4
    q_ref[...] = (q * (SCALE * LOG2E)).astype(BF16)
    kc_ref[...] = p[:, _OFF_KC:_OFF_KC + KV_W]
    vc_ref[...] = p[:, _OFF_VC:_OFF_VC + KV_W]

    ksel = p[:, _OFF_KS:_OFF_KS + KV_W] * cos + p[:, _OFF_RKS:_OFF_RKS + KV_W] * sin
    kwin = p[:, _OFF_KW:_OFF_KW + KV_W] * cos + p[:, _OFF_RKW:_OFF_RKW + KV_W] * sin

    shape = ksel.shape
    lane = lax.broadcasted_iota(jnp.int32, shape, 1)
    row = lax.broadcasted_iota(jnp.int32, shape, 0)
    lo = lane < HALF
    pos = (pl.program_id(0) % tiles_per_seq) * TM + row
    onehot = jnp.where(lane == jnp.right_shift(pos, 6), 1.0, 0.0).astype(BF16)
    ksel_sw = pltpu.roll(ksel, HALF, 1)
    kwin_sw = pltpu.roll(kwin, HALF, 1)
    for g, (ks_lo, ks_hi, kw_lo, kw_hi) in enumerate(((ksel, ksel_sw, kwin, kwin_sw),
                                                      (ksel_sw, ksel, kwin_sw, kwin))):
        ksa_ref[g] = jnp.concatenate([jnp.where(lo, ks_lo, 0.0).astype(BF16), onehot], axis=1)
        ksb_ref[g] = jnp.concatenate([jnp.where(lo, 0.0, ks_hi).astype(BF16), onehot], axis=1)
        kwa_ref[g] = jnp.where(lo, kw_lo, 0.0).astype(BF16)
        kwb_ref[g] = jnp.where(lo, 0.0, kw_hi).astype(BF16)

    vsel_t = p[:, _OFF_VS:_OFF_VS + KV_W].T
    vwin_t = p[:, _OFF_VW:_OFF_VW + KV_W].T
    ones_row = jnp.where(lax.broadcasted_iota(jnp.int32, (V_ROWS - HEAD_DIM, TM), 0) == 0, 1.0, 0.0)
    for g in range(N_KV_GROUPS):
        rows = slice(g * HEAD_DIM, (g + 1) * HEAD_DIM)
        vs_aug = jnp.concatenate([vsel_t[rows, :], ones_row], axis=0).astype(BF16)
        vw_aug = jnp.concatenate([vwin_t[rows, :], ones_row], axis=0).astype(BF16)
        vst_ref[g, 0] = vs_aug
        for r in range(TM // TQ):
            vsd_ref[g, r] = vs_aug[:, r * TQ:(r + 1) * TQ]
            vwt_ref[g, r] = vw_aug[:, r * TQ:(r + 1) * TQ]

    gn_ref[...] = jax.nn.sigmoid(p[:, _OFF_GN:_OFF_GN + 2 * LANES])

    ge = _gelu_tanh(p[:, _OFF_UV:_OFF_UV + 2 * SGU_W])
    u = ge[:, :SGU_W]
    v = _rms(ge[:, SGU_W:], sgn_ref[...])
    tri = (lax.broadcasted_iota(jnp.int32, (CHUNK, CHUNK), 0)
           >= lax.broadcasted_iota(jnp.int32, (CHUNK, CHUNK), 1))
    lo_c = lax.broadcasted_iota(jnp.int32, (CHUNK, LANES), 1) < HALF
    n_chunks = TM // CHUNK
    cols = []
    for pr in range(SGU_W // LANES):
        w_even = jnp.where(tri, ws_ref[2 * pr], 0.0).astype(BF16)
        w_odd = jnp.where(tri, ws_ref[2 * pr + 1], 0.0).astype(BF16)
        rows = []
        for r in range(n_chunks):
            vp = v[r * CHUNK:(r + 1) * CHUNK, pr * LANES:(pr + 1) * LANES]
            v_even = jnp.where(lo_c, vp, 0.0).astype(BF16)
            v_odd = jnp.where(lo_c, 0.0, vp).astype(BF16)
            rows.append(_dot(w_even, v_even) + _dot(w_odd, v_odd))
        cols.append(jnp.concatenate(rows, axis=0))
    vo = jnp.concatenate(cols, axis=1)
    bias = jnp.concatenate([bs_ref[...]] * n_chunks, axis=0)
    yb_ref[...] = (u * (vo + bias)).astype(BF16)


def _inproj(x2, g, w, cos, sin, sgn, ws, bs, seq):
    t, d = x2.shape
    assert TM == TK
    tiles_per_seq = seq // TM
    const2 = lambda i: (0, 0)
    row = lambda i: (i, 0)
    grp = lambda i: (0, i, 0)
    grp4 = lambda i: (0, i, 0, 0)
    pos = lambda i: (i % tiles_per_seq, 0)
    ng = N_KV_GROUPS
    return pl.pallas_call(
        functools.partial(_inproj_body, tiles_per_seq=tiles_per_seq),
        grid=(t // TM,),
        in_specs=[
            pl.BlockSpec((TM, d), row),
            pl.BlockSpec((1, d), const2),
            pl.BlockSpec((d, _IN_COLS), const2, pipeline_mode=pl.Buffered(1)),
            pl.BlockSpec((TM, LANES), pos),
            pl.BlockSpec((TM, LANES), pos),
            pl.BlockSpec((1, SGU_W), const2),
            pl.BlockSpec((SGU_GROUPS, CHUNK, CHUNK), lambda i: (0, 0, 0)),
            pl.BlockSpec((CHUNK, SGU_W), const2),
        ],
        out_specs=[
            pl.BlockSpec((TM, Q_W), row),
            pl.BlockSpec((TM, KV_W), row),
            pl.BlockSpec((TM, KV_W), row),
            pl.BlockSpec((ng, TM, 2 * LANES), grp),
            pl.BlockSpec((ng, TM, 2 * LANES), grp),
            pl.BlockSpec((ng, 1, V_ROWS, TK), grp4),
            pl.BlockSpec((ng, TM // TQ, V_ROWS, TQ), grp4),
            pl.BlockSpec((ng, TM, LANES), grp),
            pl.BlockSpec((ng, TM, LANES), grp),
            pl.BlockSpec((ng, TM // TQ, V_ROWS, TQ), grp4),
            pl.BlockSpec((TM, 2 * LANES), row),
            pl.BlockSpec((TM, SGU_W), row),
        ],
        out_shape=[
            jax.ShapeDtypeStruct((t, Q_W), BF16),
            jax.ShapeDtypeStruct((t, KV_W), F32),
            jax.ShapeDtypeStruct((t, KV_W), F32),
            jax.ShapeDtypeStruct((ng, t, 2 * LANES), BF16),
            jax.ShapeDtypeStruct((ng, t, 2 * LANES), BF16),
            jax.ShapeDtypeStruct((ng, t // TK, V_ROWS, TK), BF16),
            jax.ShapeDtypeStruct((ng, t // TQ, V_ROWS, TQ), BF16),
            jax.ShapeDtypeStruct((ng, t, LANES), BF16),
            jax.ShapeDtypeStruct((ng, t, LANES), BF16),
            jax.ShapeDtypeStruct((ng, t // TQ, V_ROWS, TQ), BF16),
            jax.ShapeDtypeStruct((t, 2 * LANES), F32),
            jax.ShapeDtypeStruct((t, SGU_W), BF16),
        ],
        compiler_params=_params("parallel"),
        name="inproj",
    )(x2, g, w, cos, sin, sgn, ws, bs)


def _compress_body(kc_ref, vc_ref, pekt_ref, pekb_ref, pevt_ref, pevb_ref,
                   wkt_ref, wkb_ref, wvt_ref, wvb_ref, w2k_ref, w2vt_ref, cos_ref, sin_ref,
                   kca_ref, kcb_ref, vct_ref):
    n_rows = kc_ref.shape[0]

    def hidden(c_ref, pet_ref, peb_ref, wt_ref, wb_ref):
        c = c_ref[...]
        top = _dot((c + pet_ref[...]).astype(BF16), wt_ref[...])
        bot = _dot((c + peb_ref[...]).astype(BF16), wb_ref[...])
        return _gelu_tanh(top + pltpu.roll(bot, n_rows - 1, 0))

    hk = hidden(kc_ref, pekt_ref, pekb_ref, wkt_ref, wkb_ref)
    hv = hidden(vc_ref, pevt_ref, pevb_ref, wvt_ref, wvb_ref)
    cos = cos_ref[...]
    sin = sin_ref[...]
    for g in range(N_KV_GROUPS):
        hkg = hk[:, g * PHI_HIDDEN:(g + 1) * PHI_HIDDEN].astype(BF16)
        hvg = hv[:, g * PHI_HIDDEN:(g + 1) * PHI_HIDDEN].astype(BF16)
        kk = _dot(hkg, w2k_ref[...])
        kca_ref[g] = (kk[:, 0:LANES] * cos + kk[:, 2 * LANES:3 * LANES] * sin).astype(BF16)
        kcb_ref[g] = (kk[:, LANES:2 * LANES] * cos + kk[:, 3 * LANES:4 * LANES] * sin).astype(BF16)
        vct_ref[g, 0] = _dot_nt(w2vt_ref[...], hvg).astype(BF16)


def _compress(kc_rows, vc_rows, pe, w1, w2k, w2vt, cos_c, sin_c, batch):
    total, width = kc_rows.shape
    n_rows = total // batch
    const = lambda b: (0, 0)
    k_shape = jax.ShapeDtypeStruct((N_KV_GROUPS, total, LANES), BF16)
    k_spec = pl.BlockSpec((N_KV_GROUPS, n_rows, LANES), lambda b: (0, b, 0))
    pe_spec = pl.BlockSpec((1, width), const)
    w1_spec = pl.BlockSpec((width, N_KV_GROUPS * PHI_HIDDEN), const)
    return pl.pallas_call(
        _compress_body,
        grid=(batch,),
        in_specs=[
            pl.BlockSpec((n_rows, width), lambda b: (b, 0)),
            pl.BlockSpec((n_rows, width), lambda b: (b, 0)),
            pe_spec, pe_spec, pe_spec, pe_spec,
            w1_spec, w1_spec, w1_spec, w1_spec,
            pl.BlockSpec((PHI_HIDDEN, 4 * LANES), const),
            pl.BlockSpec((HEAD_DIM, PHI_HIDDEN), const),
            pl.BlockSpec((n_rows, LANES), const),
            pl.BlockSpec((n_rows, LANES), const),
        ],
        out_specs=[k_spec, k_spec,
                   pl.BlockSpec((N_KV_GROUPS, 1, HEAD_DIM, n_rows), lambda b: (0, b, 0, 0))],
        out_shape=[k_shape, k_shape,
                   jax.ShapeDtypeStruct((N_KV_GROUPS, batch, HEAD_DIM, n_rows), BF16)],
        compiler_params=_params("parallel"),
        name="compress",
    )(kc_rows, vc_rows, *pe, *w1, w2k, w2vt, cos_c, sin_c)


CMP_CLASSES = 4
CMP_TILES = 4


def _cmp_body(q_ref, kca_ref, kcb_ref, vct_ref, gn_ref, selt_ref, o_ref, bias_ref):
    n_cmp = kca_ref.shape[1]
    n_blk = selt_ref.shape[0]
    n_col = HEADS_PER_GROUP * TQC

    def run(rows_cmp, rows_blk, k):
        t0 = (pl.program_id(2) * CMP_TILES + k) * TQC
        base = k * TQC
        lq = jnp.concatenate([q_ref[base:base + TQC, pr * LANES:(pr + 1) * LANES] for pr in range(HEAD_PAIRS)],
                             axis=0)
        tok = t0 + jnp.bitwise_and(lax.broadcasted_iota(jnp.int32, (1, n_col), 1), TQC - 1)
        any_valid = jnp.where(tok >= L_CMP - 1, 1.0, 0.0)
        s = jnp.concatenate([_dot_nt(kca_ref[0, 0:rows_cmp, :], lq),
                             _dot_nt(kcb_ref[0, 0:rows_cmp, :], lq)], axis=1)
        cmp_end = lax.broadcasted_iota(jnp.int32, (rows_cmp, 1), 0) * CMP_STRIDE + (L_CMP - 1)
        s = jnp.where(cmp_end <= tok, s, NEG_INF)
        e = jnp.exp2(s - jnp.max(s, axis=0, keepdims=True))
        p = e * (any_valid / jnp.sum(e, axis=0, keepdims=True))
        oc_t = _dot(vct_ref[0, 0, :, 0:rows_cmp], p.astype(BF16))
        psum = p[:, 0:TQC]
        for hh in range(1, HEADS_PER_GROUP):
            psum = psum + p[:, hh * TQC:(hh + 1) * TQC]

        selt = selt_ref[0:rows_blk, 0:rows_cmp]
        p_hi = psum.astype(BF16)
        r1 = psum - p_hi.astype(F32)
        p_mid = r1.astype(BF16)
        p_lo = (r1 - p_mid.astype(F32)).astype(BF16)
        imp = _dot(selt, p_hi) + _dot(selt, p_mid) + _dot(selt, p_lo)

        jj = lax.broadcasted_iota(jnp.int32, (rows_blk, TQC), 0)
        cur = jnp.right_shift(t0 + lax.broadcasted_iota(jnp.int32, (rows_blk, TQC), 1), 6)
        jf = jj.astype(F32)
        taken = -3e38
        val = jnp.where(jj < cur, imp, -FORCE)
        for _ in range(N_TOPK - 1):
            best = jnp.max(val, axis=0, keepdims=True)
            first = jnp.min(jnp.where(val == best, jf, 1e6), axis=0, keepdims=True)
            val = jnp.where(jf == first, taken, val)
        bias_t = jnp.where(jnp.logical_and(val == taken, jj < cur), 0.0, NEG_INF)
        if rows_blk < n_blk:
            bias_t = jnp.concatenate([bias_t, jnp.full((n_blk - rows_blk, TQC), NEG_INF, F32)], axis=0)

        for u in range(TQC // LANES):
            cols = slice(u * LANES, (u + 1) * LANES)
            rows = slice(base + u * LANES, base + (u + 1) * LANES)
            gn_t = gn_ref[rows, :].T
            bias_ref[0, rows, :] = bias_t[:, cols].T.astype(BF16)
            for pr in range(HEAD_PAIRS):
                c_even = pr * TQC + u * LANES
                c_odd = (HEAD_PAIRS + pr) * TQC + u * LANES
                even = oc_t[:, c_even:c_even + LANES] * gn_t[2 * pr:2 * pr + 1, :]
                odd = oc_t[:, c_odd:c_odd + LANES] * gn_t[2 * pr + 1:2 * pr + 2, :]
                o_ref[rows, pr * LANES:(pr + 1) * LANES] = jnp.concatenate([even, odd], axis=0).T

    steps_per_class = pl.num_programs(2) // CMP_CLASSES
    cls = pl.program_id(2) // steps_per_class

    def run_step(c):
        for k in range(CMP_TILES):
            run(n_cmp * (c + 1) // CMP_CLASSES, n_blk * (c + 1) // CMP_CLASSES, k)

    for c in range(CMP_CLASSES):
        pl.when(cls == c)(functools.partial(run_step, c))


def _cmp_attention(q, kca, kcb, vct, gn, selt, batch, seq):
    t = q.shape[0]
    n_cmp = kca.shape[1] // batch
    rows = CMP_TILES * TQC
    nq = seq // rows
    assert nq % CMP_CLASSES == 0
    k_spec = pl.BlockSpec((1, n_cmp, LANES), lambda b, g, i: (g, b, 0))
    return pl.pallas_call(
        _cmp_body,
        grid=(batch, N_KV_GROUPS, nq),
        in_specs=[
            pl.BlockSpec((rows, 2 * LANES), lambda b, g, i: (b * nq + i, g)),
            k_spec, k_spec,
            pl.BlockSpec((1, 1, HEAD_DIM, n_cmp), lambda b, g, i: (g, b, 0, 0)),
            pl.BlockSpec((rows, LANES), lambda b, g, i: (b * nq + i, g)),
            pl.BlockSpec(selt.shape, lambda b, g, i: (0, 0)),
        ],
        out_specs=[
            pl.BlockSpec((rows, 2 * LANES), lambda b, g, i: (b * nq + i, g)),
            pl.BlockSpec((1, rows, LANES), lambda b, g, i: (g, b * nq + i, 0)),
        ],
        out_shape=[
            jax.ShapeDtypeStruct((t, Q_W), F32),
            jax.ShapeDtypeStruct((N_KV_GROUPS, t, LANES), BF16),
        ],
        compiler_params=_params("parallel", "parallel", "parallel"),
        name="cmp_attn",
    )(q, kca, kcb, vct, gn, selt)


def _selwin_body(q_ref, bias_ref, ksa_ref, ksb_ref, vst_ref, vsd_ref, kwa_ref, kwb_ref, vwt_ref, gn_ref,
                 oc_ref, tri_ref, o_ref, *scratch):
    per_set = len(scratch) // 2
    for k in range(SELWIN_TILES):
        _selwin_tile(k, q_ref, bias_ref, ksa_ref, ksb_ref, vst_ref, vsd_ref, kwa_ref, kwb_ref, vwt_ref, gn_ref,
                     oc_ref, tri_ref, o_ref, *scratch[(k % 2) * per_set:(k % 2 + 1) * per_set])


def _selwin_tile(k, q_ref, bias_ref, ksa_ref, ksb_ref, vst_ref, vsd_ref, kwa_ref, kwb_ref, vwt_ref, gn_ref,
                 oc_ref, tri_ref, o_ref, acc_ref, sa_ref, sb_ref, pa_ref, pb_ref, sw_ref):
    qi = pl.program_id(2) * SELWIN_TILES + k
    rows = slice(k * TQ, (k + 1) * TQ)
    t0 = qi * TQ
    n_col = HEADS_PER_GROUP * TQ
    q_pairs = [q_ref[rows, pr * LANES:(pr + 1) * LANES] for pr in range(HEAD_PAIRS)]
    lq = jnp.concatenate(q_pairs, axis=0)
    bias = bias_ref[0, rows, :]
    ls = jnp.concatenate([jnp.concatenate([qp, bias], axis=1) for qp in q_pairs], axis=0)
    tok = t0 + jnp.bitwise_and(lax.broadcasted_iota(jnp.int32, (1, n_col), 1), TQ - 1)
    jd = t0 // TK

    def scores(j):
        k_lo = pl.multiple_of(j * TK, TK)
        s = jnp.concatenate([_dot_nt(ksa_ref[0, pl.ds(k_lo, TK), :], ls),
                             _dot_nt(ksb_ref[0, pl.ds(k_lo, TK), :], ls)], axis=1)
        return s, jnp.max(s, axis=0, keepdims=True)

    def weights(s, mx, m_old):
        m_new = jnp.maximum(m_old, mx)
        return m_new, jnp.exp2(m_old - m_new), jnp.exp2(s - m_new).astype(BF16)

    neg_inf = jnp.full((1, n_col), -jnp.inf, F32)
    acc_ref[...] = jnp.zeros(acc_ref.shape, F32)
    pb_ref[...] = jnp.zeros(pb_ref.shape, BF16)
    sa_ref[...], mx0 = scores(0)
    sb_ref[...], mx1 = scores(1)

    n_wt = W_WIN // TQ + 1
    win_tiles = []
    parts = []
    for r in range(n_wt):
        kt = qi - (n_wt - 1) + r
        kt_c = jnp.maximum(kt, 0)
        k_r = pl.multiple_of(kt_c * TQ, TQ)
        s_r = jnp.concatenate([_dot_nt(kwa_ref[0, pl.ds(k_r, TQ), :], lq),
                               _dot_nt(kwb_ref[0, pl.ds(k_r, TQ), :], lq)], axis=1)
        if r == 0:
            s_r = s_r + tri_ref[0]
        if r == n_wt - 1:
            s_r = s_r + tri_ref[1]
        else:
            s_r = jnp.where(kt >= 0, s_r, NEG_INF)
        parts.append(s_r)
        win_tiles.append(kt_c)
    sw = jnp.concatenate(parts, axis=0)
    mx_w = jnp.max(sw, axis=0, keepdims=True)
    sw_ref[...] = sw

    def pair(j, carry):
        m, mx_a, mx_b, a_prev = carry
        m, a_cur, pa_ref[...] = weights(sa_ref[...], mx_a, m)
        sa_ref[...], mx_a = scores(j + 2)
        acc_ref[...] = a_prev * acc_ref[...] + _dot(vst_ref[0, jnp.maximum(j - 1, 0)], pb_ref[...])
        m, a_next, pb_ref[...] = weights(sb_ref[...], mx_b, m)
        sb_ref[...], mx_b = scores(j + 3)
        acc_ref[...] = a_cur * acc_ref[...] + _dot(vst_ref[0, j], pa_ref[...])
        return m, mx_a, mx_b, a_next

    carry = (neg_inf, mx0, mx1, jnp.ones((1, n_col), F32))
    m, mx_a, mx_b, a_prev = lax.fori_loop(0, jd // 2, lambda i, c: pair(2 * i, c), carry)

    jf = 2 * (jd // 2)
    jd_even = ((k * TQ) // TK) % 2 == 0
    m, a_cur, pa_ref[...] = weights(sa_ref[...], mx_a, m)
    acc_ref[...] = a_prev * acc_ref[...] + _dot(vst_ref[0, jnp.maximum(jf - 1, 0)], pb_ref[...])
    if jd_even:
        o_sel = a_cur * acc_ref[...] + _dot(vst_ref[0, jf], pa_ref[...])
    else:
        m, a_next, pb_ref[...] = weights(sb_ref[...], mx_b, m)
        acc_ref[...] = a_cur * acc_ref[...] + _dot(vst_ref[0, jf], pa_ref[...])
        o_sel = a_next * acc_ref[...] + _dot(vst_ref[0, jf + 1], pb_ref[...])
    kd = pl.multiple_of(t0, TQ)
    sd = jnp.concatenate([_dot_nt(ksa_ref[0, pl.ds(kd, TQ), 0:LANES], lq),
                          _dot_nt(ksb_ref[0, pl.ds(kd, TQ), 0:LANES], lq)], axis=1)
    key_d = t0 + lax.broadcasted_iota(jnp.int32, (TQ, 1), 0)
    own = jnp.logical_and(jnp.right_shift(key_d, 6) == jnp.right_shift(tok, 6), key_d <= tok)
    sd = jnp.where(own, sd, NEG_INF)
    _, a_own, pd = weights(sd, jnp.max(sd, axis=0, keepdims=True), m)
    o_sel = a_own * o_sel + _dot(vsd_ref[0, qi], pd)
    pw = jnp.exp2(sw_ref[...] - mx_w).astype(BF16)
    o_win = _dot(vwt_ref[0, win_tiles[0]], pw[0:TQ])
    for r in range(1, n_wt):
        o_win = o_win + _dot(vwt_ref[0, win_tiles[r]], pw[r * TQ:(r + 1) * TQ])
    os_t = o_sel[:HEAD_DIM] * (1.0 / o_sel[HEAD_DIM:HEAD_DIM + 1])
    ow_t = o_win[:HEAD_DIM] * (1.0 / o_win[HEAD_DIM:HEAD_DIM + 1])

    gn_t = gn_ref[rows, :].T
    for pr in range(HEAD_PAIRS):
        halves = []
        for e in range(2):
            c0 = (e * HEAD_PAIRS + pr) * TQ
            hh = 2 * pr + e
            g_sel = gn_t[HEADS_PER_GROUP + hh:HEADS_PER_GROUP + hh + 1, :]
            g_win = gn_t[2 * HEADS_PER_GROUP + hh:2 * HEADS_PER_GROUP + hh + 1, :]
            halves.append(os_t[:, c0:c0 + TQ] * g_sel + ow_t[:, c0:c0 + TQ] * g_win)
        cols = slice(pr * LANES, (pr + 1) * LANES)
        o_ref[rows, cols] = (jnp.concatenate(halves, axis=0).T + oc_ref[rows, cols]).astype(BF16)


def _selwin_attention(q, bias, ksa, ksb, vst, vsd, kwa, kwb, vwt, gn, oc, tri, batch, seq):
    t = q.shape[0]
    rows = SELWIN_TILES * TQ
    nq = seq // rows
    n_col = HEADS_PER_GROUP * TQ
    tile_row = lambda b, g, i: (b * nq + i, g)
    grp = lambda b, g, i: (g, b, 0)
    grp4 = lambda b, g, i: (g, b, 0, 0)
    once = pl.Buffered(1)
    scratch_set = [pltpu.VMEM((V_ROWS, n_col), F32),
                   pltpu.VMEM((TK, n_col), F32), pltpu.VMEM((TK, n_col), F32),
                   pltpu.VMEM((TK, n_col), BF16), pltpu.VMEM((TK, n_col), BF16),
                   pltpu.VMEM((W_WIN + TQ, n_col), F32)]
    return pl.pallas_call(
        _selwin_body,
        grid=(batch, N_KV_GROUPS, nq),
        in_specs=[
            pl.BlockSpec((rows, 2 * LANES), tile_row),
            pl.BlockSpec((1, rows, LANES), lambda b, g, i: (g, b * nq + i, 0)),
            pl.BlockSpec((1, seq, 2 * LANES), grp, pipeline_mode=once),
            pl.BlockSpec((1, seq, 2 * LANES), grp, pipeline_mode=once),
            pl.BlockSpec((1, seq // TK, V_ROWS, TK), grp4, pipeline_mode=once),
            pl.BlockSpec((1, seq // TQ, V_ROWS, TQ), grp4, pipeline_mode=once),
            pl.BlockSpec((1, seq, LANES), grp, pipeline_mode=once),
            pl.BlockSpec((1, seq, LANES), grp, pipeline_mode=once),
            pl.BlockSpec((1, seq // TQ, V_ROWS, TQ), grp4, pipeline_mode=once),
            pl.BlockSpec((rows, LANES), tile_row),
            pl.BlockSpec((rows, 2 * LANES), tile_row),
            pl.BlockSpec(tri.shape, lambda b, g, i: (0, 0, 0), pipeline_mode=once),
        ],
        out_specs=pl.BlockSpec((rows, 2 * LANES), tile_row),
        out_shape=jax.ShapeDtypeStruct((t, Q_W), BF16),
        scratch_shapes=scratch_set + scratch_set,
        compiler_params=_params("parallel", "parallel", "arbitrary"),
        name="selwin_attn",
    )(q, bias, ksa, ksb, vst, vsd, kwa, kwb, vwt, gn, oc, tri)


def _merge_body(x_ref, g_ref, wg_ref, ya_ref, yb_ref, pa_ref, pb_ref, wo_ref, o_ref):
    x = x_ref[...]
    d = x.shape[1]
    h = _rms(x, g_ref[...]).astype(BF16)
    gates = jax.nn.sigmoid(_dot(h, wg_ref[...]))
    merged = gates[:, :d] * _dot(ya_ref[...], pa_ref[...]) + gates[:, d:] * _dot(yb_ref[...], pb_ref[...])
    o_ref[...] = x + _dot(merged.astype(BF16), wo_ref[...])


def _merge(x2, g, wg, ya, yb, pa, pb, wo):
    t, d = x2.shape
    const = lambda i: (0, 0)
    row = lambda i: (i, 0)
    return pl.pallas_call(
        _merge_body,
        grid=(t // TM,),
        in_specs=[
            pl.BlockSpec((TM, d), row),
            pl.BlockSpec((1, d), const),
            pl.BlockSpec((d, 2 * d), const),
            pl.BlockSpec((TM, Q_W), row),
            pl.BlockSpec((TM, SGU_W), row),
            pl.BlockSpec((Q_W, d), const),
            pl.BlockSpec((SGU_W, d), const),
            pl.BlockSpec((d, d), const),
        ],
        out_specs=pl.BlockSpec((TM, d), row),
        out_shape=jax.ShapeDtypeStruct((t, d), F32),
        compiler_params=_params("parallel"),
        name="merge",
    )(x2, g, wg, ya, yb, pa, pb, wo)


def _rot_cols(w, heads):
    d = w.shape[0]
    w4 = w.reshape(d, heads, 2, HEAD_DIM // 2)
    return jnp.concatenate([-w4[:, :, 1], w4[:, :, 0]], axis=-1).reshape(d, heads * HEAD_DIM)


def _rope_tables(pos):
    inv = ROPE_THETA ** (-jnp.arange(0, HEAD_DIM, 2, dtype=F32) / HEAD_DIM)
    ang = pos.astype(F32)[:, None] * inv[None, :]
    cos = jnp.concatenate([jnp.cos(ang)] * (2 * LANES // HEAD_DIM), axis=-1)
    sin = jnp.concatenate([jnp.sin(ang)] * (2 * LANES // HEAD_DIM), axis=-1)
    return cos, sin


def _inproj_weight(w_in):
    q = w_in[:, 0:Q_W]
    ks = w_in[:, _OFF_KS:_OFF_KS + KV_W]
    kw = w_in[:, _OFF_KW:_OFF_KW + KV_W]
    gn0 = Q_W + 6 * KV_W
    zero = jnp.zeros((w_in.shape[0], LANES - 3 * HEADS_PER_GROUP), w_in.dtype)
    gn = []
    for g in range(N_KV_GROUPS):
        w_g = w_in[:, gn0 + g * 3 * HEADS_PER_GROUP:gn0 + (g + 1) * 3 * HEADS_PER_GROUP]
        gn += [w_g.reshape(-1, HEADS_PER_GROUP, 3).transpose(0, 2, 1).reshape(-1, 3 * HEADS_PER_GROUP), zero]
    uv0 = gn0 + NSA_GATE_W
    uv = w_in[:, uv0:uv0 + 2 * SGU_W]
    w = jnp.concatenate([w_in[:, :gn0], _rot_cols(q, N_HEADS), _rot_cols(ks, N_KV_GROUPS),
                         _rot_cols(kw, N_KV_GROUPS), *gn, uv], axis=1)
    assert w.shape[1] == _IN_COLS
    return w.astype(BF16), w_in[:, uv0 + 2 * SGU_W:].astype(BF16)


def _compress_params(pe_k, pe_v, w1k, w1v, w2k, w2v):
    half = L_CMP // 2

    def pe_rows(pe):
        rep = jnp.broadcast_to(pe[:, None, :], (L_CMP, N_KV_GROUPS, HEAD_DIM))
        return (rep[:half].reshape(1, -1), rep[half:].reshape(1, -1))

    def w1_halves(w1):
        w = w1.reshape(L_CMP, HEAD_DIM, PHI_HIDDEN)
        zero = jnp.zeros((half, HEAD_DIM, PHI_HIDDEN), w.dtype)
        outs = []
        for part in (w[:half], w[half:]):
            blocks = jnp.stack([jnp.concatenate([part, zero], axis=-1),
                                jnp.concatenate([zero, part], axis=-1)], axis=1)
            outs.append(blocks.reshape(half * N_KV_GROUPS * HEAD_DIM, N_KV_GROUPS * PHI_HIDDEN).astype(BF16))
        return outs

    zero = jnp.zeros_like(w2k)
    w2k_rot = _rot_cols(w2k, 1)
    w2k_all = jnp.concatenate([w2k, zero, zero, w2k, w2k_rot, zero, zero, w2k_rot], axis=1).astype(BF16)
    pkt, pkb = pe_rows(pe_k)
    pvt, pvb = pe_rows(pe_v)
    return (pkt, pkb, pvt, pvb), (*w1_halves(w1k), *w1_halves(w1v)), w2k_all, w2v.T.astype(BF16)


def _window_triangles():
    key = np.arange(TQ)[:, None]
    tok = np.arange(HEADS_PER_GROUP * TQ)[None, :] % TQ
    upper = np.where(key > tok, 0.0, NEG_INF)
    lower = np.where(key <= tok, 0.0, NEG_INF)
    return jnp.asarray(np.stack([upper, lower]), dtype=F32)


def _sel_map_t(seq):
    n_rows = seq // CMP_STRIDE
    n_cmp = (seq - L_CMP) // CMP_STRIDE + 1
    cmp_start = np.arange(n_rows) * CMP_STRIDE
    sel_start = np.arange(LANES) * L_SEL
    ov = (np.minimum(cmp_start[None, :] + L_CMP, sel_start[:, None] + L_SEL)
          - np.maximum(cmp_start[None, :], sel_start[:, None]))
    m = np.clip(ov, 0, None).astype(np.float32) / L_CMP
    m[:, n_cmp:] = 0.0
    m[seq // L_SEL:, :] = 0.0
    return jnp.asarray(m, dtype=BF16)


def kernel(x, ffn1_norm, ffn1_w_gate_up, ffn1_w_down, mix_norm, w_in, cmp_pos_k, cmp_pos_v,
           phi_k_w1, phi_k_w2, phi_v_w1, phi_v_w2, sgu_norm, sgu_w_s, sgu_b_s, proj_a, proj_b,
           w_out, ffn2_norm, ffn2_w_gate_up, ffn2_w_down, final_norm):
    batch, seq, d = x.shape
    depth = w_in.shape[0]
    assert seq % TM == 0 and seq % TK == 0 and seq >= N_TOPK * L_SEL and seq // L_SEL <= LANES
    assert seq % (2 * TK) == 0 and seq >= W_WIN + TQ and W_WIN % TQ == 0 and seq % (SELWIN_TILES * TQ) == 0 and (SELWIN_TILES * TQ) % (2 * TK) == 0 and N_KV_GROUPS == 2 and HEADS_PER_GROUP == 4
    t = batch * seq
    n_rows = seq // CMP_STRIDE

    cos, sin = _rope_tables(jnp.arange(seq))
    cos_c, sin_c = _rope_tables(jnp.arange(n_rows) * CMP_STRIDE + (L_CMP - 1))
    selt = _sel_map_t(seq)
    tri = _window_triangles()
    row = lambda v: v.reshape(1, -1)

    x2 = x.reshape(t, d)
    for l in range(depth):
        x2 = _ffn(x2, row(ffn1_norm[l]), ffn1_w_gate_up[l].astype(BF16), ffn1_w_down[l].astype(BF16),
                  row(final_norm), False)

        w_main, w_gates = _inproj_weight(w_in[l])
        bs = jnp.repeat(sgu_b_s[l].T, SGU_HEAD, axis=1)
        (q, kc, vc, ksa, ksb, vst, vsd, kwa, kwb, vwt, gn, yb) = _inproj(
            x2, row(mix_norm[l]), w_main, cos, sin, row(sgu_norm[l]), sgu_w_s[l], bs, seq)

        pe, w1, w2k_all, w2vt = _compress_params(
            cmp_pos_k[l], cmp_pos_v[l], phi_k_w1[l], phi_v_w1[l], phi_k_w2[l], phi_v_w2[l])
        width = CMP_STRIDE * KV_W
        kca, kcb, vct = _compress(kc.reshape(t // CMP_STRIDE, width), vc.reshape(t // CMP_STRIDE, width),
                                  pe, w1, w2k_all, w2vt, cos_c, sin_c, batch)

        o_cmp, bias = _cmp_attention(q, kca, kcb, vct, gn, selt, batch, seq)
        ya = _selwin_attention(q, bias, ksa, ksb, vst, vsd, kwa, kwb, vwt, gn, o_cmp, tri, batch, seq)

        x2 = _merge(x2, row(mix_norm[l]), w_gates, ya, yb,
                    proj_a[l].astype(BF16), proj_b[l].astype(BF16), w_out[l].astype(BF16))

        x2 = _ffn(x2, row(ffn2_norm[l]), ffn2_w_gate_up[l].astype(BF16), ffn2_w_down[l].astype(BF16),
                  row(final_norm), l == depth - 1)
    return x2.reshape(batch, seq, d)
```

```python
import functools

import jax
import jax.numpy as jnp
import numpy as np
from jax import lax
from jax.experimental import pallas as pl
from jax.experimental.pallas import tpu as pltpu

HEAD_DIM = 64
N_HEADS = 8
N_KV_GROUPS = 2
HEADS_PER_GROUP = N_HEADS // N_KV_GROUPS
HEAD_PAIRS = HEADS_PER_GROUP // 2
Q_W = N_HEADS * HEAD_DIM
KV_W = N_KV_GROUPS * HEAD_DIM
NSA_GATE_W = N_HEADS * 3
L_CMP = 32
CMP_STRIDE = 16
L_SEL = 64
N_TOPK = 16
W_WIN = 512
PHI_HIDDEN = 256
SGU_GROUPS = 8
SGU_HEAD = 64
SGU_W = SGU_GROUPS * SGU_HEAD
CHUNK = 128
ROPE_THETA = 10000.0
EPS = 1e-6
NEG_INF = -1e30
FORCE = 1e9
SCALE = HEAD_DIM ** -0.5
LOG2E = float(np.log2(np.e))
V_ROWS = HEAD_DIM + 16

LANES = 128
HALF = LANES // 2
VMEM_LIMIT = 56 * 1024 * 1024

TM = 512
TQ = 256
TQC = 256
TK = 512
SELWIN_TILES = 4
FF_CHUNK = 256

F32 = jnp.float32
BF16 = jnp.bfloat16


def _dot(a, b):
    return jnp.dot(a, b, preferred_element_type=F32)


def _dot_nt(a, b):
    return lax.dot_general(a, b, (((1,), (1,)), ((), ())), preferred_element_type=F32)


def _rms(x, g):
    return x * lax.rsqrt(jnp.mean(x * x, axis=-1, keepdims=True) + EPS) * g


def _gelu_tanh(x):
    c = np.float32(np.sqrt(2.0 / np.pi))
    return 0.5 * x * (1.0 + jnp.tanh(c * (x + 0.044715 * (x * x * x))))


def _params(*sem):
    return pltpu.CompilerParams(dimension_semantics=sem, vmem_limit_bytes=VMEM_LIMIT)


def _ffn_body(x_ref, g_ref, wgu_ref, wd_ref, fg_ref, o_ref, *, d_ff, final_norm):
    x = x_ref[...]
    h = _rms(x, g_ref[...]).astype(BF16)
    acc = jnp.zeros(x.shape, F32)
    for c in range(d_ff // FF_CHUNK):
        lo = c * FF_CHUNK
        gate = _dot(h, wgu_ref[:, lo:lo + FF_CHUNK])
        up = _dot(h, wgu_ref[:, d_ff + lo:d_ff + lo + FF_CHUNK])
        act = (gate * jax.nn.sigmoid(gate) * up).astype(BF16)
        acc = acc + _dot(act, wd_ref[lo:lo + FF_CHUNK, :])
    y = x + 0.5 * acc
    if final_norm:
        y = _rms(y, fg_ref[...])
    o_ref[...] = y


def _ffn(x2, g, wgu, wd, fg, final_norm):
    t, d = x2.shape
    d_ff = wd.shape[0]
    assert t % TM == 0 and d_ff % FF_CHUNK == 0
    const = lambda i: (0, 0)
    return pl.pallas_call(
        functools.partial(_ffn_body, d_ff=d_ff, final_norm=final_norm),
        grid=(t // TM,),
        in_specs=[
            pl.BlockSpec((TM, d), lambda i: (i, 0)),
            pl.BlockSpec((1, d), const),
            pl.BlockSpec((d, 2 * d_ff), const, pipeline_mode=pl.Buffered(1)),
            pl.BlockSpec((d_ff, d), const, pipeline_mode=pl.Buffered(1)),
            pl.BlockSpec((1, d), const),
        ],
        out_specs=pl.BlockSpec((TM, d), lambda i: (i, 0)),
        out_shape=jax.ShapeDtypeStruct((t, d), F32),
        compiler_params=_params("parallel"),
        name="ffn",
    )(x2, g, wgu, wd, fg)


_OFF_Q, _OFF_KC, _OFF_VC, _OFF_KS, _OFF_VS, _OFF_KW, _OFF_VW = 0, 512, 640, 768, 896, 1024, 1152
_OFF_RQ, _OFF_RKS, _OFF_RKW, _OFF_GN, _OFF_UV = 1280, 1792, 1920, 2048, 2304
_IN_COLS = _OFF_UV + 2 * SGU_W


def _inproj_body(x_ref, g_ref, w_ref, cos_ref, sin_ref, sgn_ref, ws_ref, bs_ref,
                 q_ref, kc_ref, vc_ref, ksa_ref, ksb_ref, vst_ref, vsd_ref, kwa_ref, kwb_ref, vwt_ref,
                 gn_ref, yb_ref, *, tiles_per_seq):
    h = _rms(x_ref[...], g_ref[...]).astype(BF16)
    p = _dot(h, w_ref[...])
    cos = cos_ref[...]
    sin = sin_ref[...]
    cos4 = jnp.concatenate([cos] * (Q_W // LANES), axis=1)
    sin4 = jnp.concatenate([sin] * (Q_W // LANES), axis=1)

    q = p[:, _OFF_Q:_OFF_Q + Q_W] * cos4 + p[:, _OFF_RQ:_OFF_RQ + Q_W] * sin4
    q_ref[...] = (q * (SCALE * LOG2E)).astype(BF16)
    kc_ref[...] = p[:, _OFF_KC:_OFF_KC + KV_W]
    vc_ref[...] = p[:, _OFF_VC:_OFF_VC + KV_W]

    ksel = p[:, _OFF_KS:_OFF_KS + KV_W] * cos + p[:, _OFF_RKS:_OFF_RKS + KV_W] * sin
    kwin = p[:, _OFF_KW:_OFF_KW + KV_W] * cos + p[:, _OFF_RKW:_OFF_RKW + KV_W] * sin

    shape = ksel.shape
    lane = lax.broadcasted_iota(jnp.int32, shape, 1)
    row = lax.broadcasted_iota(jnp.int32, shape, 0)
    lo = lane < HALF
    pos = (pl.program_id(0) % tiles_per_seq) * TM + row
    onehot = jnp.where(lane == jnp.right_shift(pos, 6), 1.0, 0.0).astype(BF16)
    ksel_sw = pltpu.roll(ksel, HALF, 1)
    kwin_sw = pltpu.roll(kwin, HALF, 1)
    for g, (ks_lo, ks_hi, kw_lo, kw_hi) in enumerate(((ksel, ksel_sw, kwin, kwin_sw),
                                                      (ksel_sw, ksel, kwin_sw, kwin))):
        ksa_ref[g] = jnp.concatenate([jnp.where(lo, ks_lo, 0.0).astype(BF16), onehot], axis=1)
        ksb_ref[g] = jnp.concatenate([jnp.where(lo, 0.0, ks_hi).astype(BF16), onehot], axis=1)
        kwa_ref[g] = jnp.where(lo, kw_lo, 0.0).astype(BF16)
        kwb_ref[g] = jnp.where(lo, 0.0, kw_hi).astype(BF16)

    vsel_t = p[:, _OFF_VS:_OFF_VS + KV_W].T
    vwin_t = p[:, _OFF_VW:_OFF_VW + KV_W].T
    ones_row = jnp.where(lax.broadcasted_iota(jnp.int32, (V_ROWS - HEAD_DIM, TM), 0) == 0, 1.0, 0.0)
    for g in range(N_KV_GROUPS):
        rows = slice(g * HEAD_DIM, (g + 1) * HEAD_DIM)
        vs_aug = jnp.concatenate([vsel_t[rows, :], ones_row], axis=0).astype(BF16)
        vw_aug = jnp.concatenate([vwin_t[rows, :], ones_row], axis=0).astype(BF16)
        vst_ref[g, 0] = vs_aug
        for r in range(TM // TQ):
            vsd_ref[g, r] = vs_aug[:, r * TQ:(r + 1) * TQ]
            vwt_ref[g, r] = vw_aug[:, r * TQ:(r + 1) * TQ]

    gn_ref[...] = jax.nn.sigmoid(p[:, _OFF_GN:_OFF_GN + 2 * LANES])

    ge = _gelu_tanh(p[:, _OFF_UV:_OFF_UV + 2 * SGU_W])
    u = ge[:, :SGU_W]
    v = _rms(ge[:, SGU_W:], sgn_ref[...])
    tri = (lax.broadcasted_iota(jnp.int32, (CHUNK, CHUNK), 0)
           >= lax.broadcasted_iota(jnp.int32, (CHUNK, CHUNK), 1))
    lo_c = lax.broadcasted_iota(jnp.int32, (CHUNK, LANES), 1) < HALF
    n_chunks = TM // CHUNK
    cols = []
    for pr in range(SGU_W // LANES):
        w_even = jnp.where(tri, ws_ref[2 * pr], 0.0).astype(BF16)
        w_odd = jnp.where(tri, ws_ref[2 * pr + 1], 0.0).astype(BF16)
        rows = []
        for r in range(n_chunks):
            vp = v[r * CHUNK:(r + 1) * CHUNK, pr * LANES:(pr + 1) * LANES]
            v_even = jnp.where(lo_c, vp, 0.0).astype(BF16)
            v_odd = jnp.where(lo_c, 0.0, vp).astype(BF16)
            rows.append(_dot(w_even, v_even) + _dot(w_odd, v_odd))
        cols.append(jnp.concatenate(rows, axis=0))
    vo = jnp.concatenate(cols, axis=1)
    bias = jnp.concatenate([bs_ref[...]] * n_chunks, axis=0)
    yb_ref[...] = (u * (vo + bias)).astype(BF16)


def _inproj(x2, g, w, cos, sin, sgn, ws, bs, seq):
    t, d = x2.shape
    assert TM == TK
    tiles_per_seq = seq // TM
    const2 = lambda i: (0, 0)
    row = lambda i: (i, 0)
    grp = lambda i: (0, i, 0)
    grp4 = lambda i: (0, i, 0, 0)
    pos = lambda i: (i % tiles_per_seq, 0)
    ng = N_KV_GROUPS
    return pl.pallas_call(
        functools.partial(_inproj_body, tiles_per_seq=tiles_per_seq),
        grid=(t // TM,),
        in_specs=[
            pl.BlockSpec((TM, d), row),
            pl.BlockSpec((1, d), const2),
            pl.BlockSpec((d, _IN_COLS), const2, pipeline_mode=pl.Buffered(1)),
            pl.BlockSpec((TM, LANES), pos),
            pl.BlockSpec((TM, LANES), pos),
            pl.BlockSpec((1, SGU_W), const2),
            pl.BlockSpec((SGU_GROUPS, CHUNK, CHUNK), lambda i: (0, 0, 0)),
            pl.BlockSpec((CHUNK, SGU_W), const2),
        ],
        out_specs=[
            pl.BlockSpec((TM, Q_W), row),
            pl.BlockSpec((TM, KV_W), row),
            pl.BlockSpec((TM, KV_W), row),
            pl.BlockSpec((ng, TM, 2 * LANES), grp),
            pl.BlockSpec((ng, TM, 2 * LANES), grp),
            pl.BlockSpec((ng, 1, V_ROWS, TK), grp4),
            pl.BlockSpec((ng, TM // TQ, V_ROWS, TQ), grp4),
            pl.BlockSpec((ng, TM, LANES), grp),
            pl.BlockSpec((ng, TM, LANES), grp),
            pl.BlockSpec((ng, TM // TQ, V_ROWS, TQ), grp4),
            pl.BlockSpec((TM, 2 * LANES), row),
            pl.BlockSpec((TM, SGU_W), row),
        ],
        out_shape=[
            jax.ShapeDtypeStruct((t, Q_W), BF16),
            jax.ShapeDtypeStruct((t, KV_W), F32),
            jax.ShapeDtypeStruct((t, KV_W), F32),
            jax.ShapeDtypeStruct((ng, t, 2 * LANES), BF16),
            jax.ShapeDtypeStruct((ng, t, 2 * LANES), BF16),
            jax.ShapeDtypeStruct((ng, t // TK, V_ROWS, TK), BF16),
            jax.ShapeDtypeStruct((ng, t // TQ, V_ROWS, TQ), BF16),
            jax.ShapeDtypeStruct((ng, t, LANES), BF16),
            jax.ShapeDtypeStruct((ng, t, LANES), BF16),
            jax.ShapeDtypeStruct((ng, t // TQ, V_ROWS, TQ), BF16),
            jax.ShapeDtypeStruct((t, 2 * LANES), F32),
            jax.ShapeDtypeStruct((t, SGU_W), BF16),
        ],
        compiler_params=_params("parallel"),
        name="inproj",
    )(x2, g, w, cos, sin, sgn, ws, bs)


def _compress_body(kc_ref, vc_ref, pekt_ref, pekb_ref, pevt_ref, pevb_ref,
                   wkt_ref, wkb_ref, wvt_ref, wvb_ref, w2k_ref, w2vt_ref, cos_ref, sin_ref,
                   kca_ref, kcb_ref, vct_ref):
    n_rows = kc_ref.shape[0]

    def hidden(c_ref, pet_ref, peb_ref, wt_ref, wb_ref):
        c = c_ref[...]
        top = _dot((c + pet_ref[...]).astype(BF16), wt_ref[...])
        bot = _dot((c + peb_ref[...]).astype(BF16), wb_ref[...])
        return _gelu_tanh(top + pltpu.roll(bot, n_rows - 1, 0))

    hk = hidden(kc_ref, pekt_ref, pekb_ref, wkt_ref, wkb_ref)
    hv = hidden(vc_ref, pevt_ref, pevb_ref, wvt_ref, wvb_ref)
    cos = cos_ref[...]
    sin = sin_ref[...]
    for g in range(N_KV_GROUPS):
        hkg = hk[:, g * PHI_HIDDEN:(g + 1) * PHI_HIDDEN].astype(BF16)
        hvg = hv[:, g * PHI_HIDDEN:(g + 1) * PHI_HIDDEN].astype(BF16)
        kk = _dot(hkg, w2k_ref[...])
        kca_ref[g] = (kk[:, 0:LANES] * cos + kk[:, 2 * LANES:3 * LANES] * sin).astype(BF16)
        kcb_ref[g] = (kk[:, LANES:2 * LANES] * cos + kk[:, 3 * LANES:4 * LANES] * sin).astype(BF16)
        vct_ref[g, 0] = _dot_nt(w2vt_ref[...], hvg).astype(BF16)


def _compress(kc_rows, vc_rows, pe, w1, w2k, w2vt, cos_c, sin_c, batch):
    total, width = kc_rows.shape
    n_rows = total // batch
    const = lambda b: (0, 0)
    k_shape = jax.ShapeDtypeStruct((N_KV_GROUPS, total, LANES), BF16)
    k_spec = pl.BlockSpec((N_KV_GROUPS, n_rows, LANES), lambda b: (0, b, 0))
    pe_spec = pl.BlockSpec((1, width), const)
    w1_spec = pl.BlockSpec((width, N_KV_GROUPS * PHI_HIDDEN), const)
    return pl.pallas_call(
        _compress_body,
        grid=(batch,),
        in_specs=[
            pl.BlockSpec((n_rows, width), lambda b: (b, 0)),
            pl.BlockSpec((n_rows, width), lambda b: (b, 0)),
            pe_spec, pe_spec, pe_spec, pe_spec,
            w1_spec, w1_spec, w1_spec, w1_spec,
            pl.BlockSpec((PHI_HIDDEN, 4 * LANES), const),
            pl.BlockSpec((HEAD_DIM, PHI_HIDDEN), const),
            pl.BlockSpec((n_rows, LANES), const),
            pl.BlockSpec((n_rows, LANES), const),
        ],
        out_specs=[k_spec, k_spec,
                   pl.BlockSpec((N_KV_GROUPS, 1, HEAD_DIM, n_rows), lambda b: (0, b, 0, 0))],
        out_shape=[k_shape, k_shape,
                   jax.ShapeDtypeStruct((N_KV_GROUPS, batch, HEAD_DIM, n_rows), BF16)],
        compiler_params=_params("parallel"),
        name="compress",
    )(kc_rows, vc_rows, *pe, *w1, w2k, w2vt, cos_c, sin_c)


CMP_CLASSES = 4
CMP_TILES = 4


def _cmp_body(q_ref, kca_ref, kcb_ref, vct_ref, gn_ref, selt_ref, o_ref, bias_ref):
    n_cmp = kca_ref.shape[1]
    n_blk = selt_ref.shape[0]
    n_col = HEADS_PER_GROUP * TQC

    def run(rows_cmp, rows_blk, k):
        t0 = (pl.program_id(2) * CMP_TILES + k) * TQC
        base = k * TQC
        lq = jnp.concatenate([q_ref[base:base + TQC, pr * LANES:(pr + 1) * LANES] for pr in range(HEAD_PAIRS)],
                             axis=0)
        tok = t0 + jnp.bitwise_and(lax.broadcasted_iota(jnp.int32, (1, n_col), 1), TQC - 1)
        any_valid = jnp.where(tok >= L_CMP - 1, 1.0, 0.0)
        s = jnp.concatenate([_dot_nt(kca_ref[0, 0:rows_cmp, :], lq),
                             _dot_nt(kcb_ref[0, 0:rows_cmp, :], lq)], axis=1)
        cmp_end = lax.broadcasted_iota(jnp.int32, (rows_cmp, 1), 0) * CMP_STRIDE + (L_CMP - 1)
        s = jnp.where(cmp_end <= tok, s, NEG_INF)
        e = jnp.exp2(s - jnp.max(s, axis=0, keepdims=True))
        p = e * (any_valid / jnp.sum(e, axis=0, keepdims=True))
        oc_t = _dot(vct_ref[0, 0, :, 0:rows_cmp], p.astype(BF16))
        psum = p[:, 0:TQC]
        for hh in range(1, HEADS_PER_GROUP):
            psum = psum + p[:, hh * TQC:(hh + 1) * TQC]

        selt = selt_ref[0:rows_blk, 0:rows_cmp]
        p_hi = psum.astype(BF16)
        r1 = psum - p_hi.astype(F32)
        p_mid = r1.astype(BF16)
        p_lo = (r1 - p_mid.astype(F32)).astype(BF16)
        imp = _dot(selt, p_hi) + _dot(selt, p_mid) + _dot(selt, p_lo)

        jj = lax.broadcasted_iota(jnp.int32, (rows_blk, TQC), 0)
        cur = jnp.right_shift(t0 + lax.broadcasted_iota(jnp.int32, (rows_blk, TQC), 1), 6)
        jf = jj.astype(F32)
        taken = -3e38
        val = jnp.where(jj < cur, imp, -FORCE)
        for _ in range(N_TOPK - 1):
            best = jnp.max(val, axis=0, keepdims=True)
            first = jnp.min(jnp.where(val == best, jf, 1e6), axis=0, keepdims=True)
            val = jnp.where(jf == first, taken, val)
        bias_t = jnp.where(jnp.logical_and(val == taken, jj < cur), 0.0, NEG_INF)
        if rows_blk < n_blk:
            bias_t = jnp.concatenate([bias_t, jnp.full((n_blk - rows_blk, TQC), NEG_INF, F32)], axis=0)

        for u in range(TQC // LANES):
            cols = slice(u * LANES, (u + 1) * LANES)
            rows = slice(base + u * LANES, base + (u + 1) * LANES)
            gn_t = gn_ref[rows, :].T
            bias_ref[0, rows, :] = bias_t[:, cols].T.astype(BF16)
            for pr in range(HEAD_PAIRS):
                c_even = pr * TQC + u * LANES
                c_odd = (HEAD_PAIRS + pr) * TQC + u * LANES
                even = oc_t[:, c_even:c_even + LANES] * gn_t[2 * pr:2 * pr + 1, :]
                odd = oc_t[:, c_odd:c_odd + LANES] * gn_t[2 * pr + 1:2 * pr + 2, :]
                o_ref[rows, pr * LANES:(pr + 1) * LANES] = jnp.concatenate([even, odd], axis=0).T

    steps_per_class = pl.num_programs(2) // CMP_CLASSES
    cls = pl.program_id(2) // steps_per_class

    def run_step(c):
        for k in range(CMP_TILES):
            run(n_cmp * (c + 1) // CMP_CLASSES, n_blk * (c + 1) // CMP_CLASSES, k)

    for c in range(CMP_CLASSES):
        pl.when(cls == c)(functools.partial(run_step, c))


def _cmp_attention(q, kca, kcb, vct, gn, selt, batch, seq):
    t = q.shape[0]
    n_cmp = kca.shape[1] // batch
    rows = CMP_TILES * TQC
    nq = seq // rows
    assert nq % CMP_CLASSES == 0
    k_spec = pl.BlockSpec((1, n_cmp, LANES), lambda b, g, i: (g, b, 0))
    return pl.pallas_call(
        _cmp_body,
        grid=(batch, N_KV_GROUPS, nq),
        in_specs=[
            pl.BlockSpec((rows, 2 * LANES), lambda b, g, i: (b * nq + i, g)),
            k_spec, k_spec,
            pl.BlockSpec((1, 1, HEAD_DIM, n_cmp), lambda b, g, i: (g, b, 0, 0)),
            pl.BlockSpec((rows, LANES), lambda b, g, i: (b * nq + i, g)),
            pl.BlockSpec(selt.shape, lambda b, g, i: (0, 0)),
        ],
        out_specs=[
            pl.BlockSpec((rows, 2 * LANES), lambda b, g, i: (b * nq + i, g)),
            pl.BlockSpec((1, rows, LANES), lambda b, g, i: (g, b * nq + i, 0)),
        ],
        out_shape=[
            jax.ShapeDtypeStruct((t, Q_W), F32),
            jax.ShapeDtypeStruct((N_KV_GROUPS, t, LANES), BF16),
        ],
        compiler_params=_params("parallel", "parallel", "parallel"),
        name="cmp_attn",
    )(q, kca, kcb, vct, gn, selt)


def _selwin_body(q_ref, bias_ref, ksa_ref, ksb_ref, vst_ref, vsd_ref, kwa_ref, kwb_ref, vwt_ref, gn_ref,
                 oc_ref, tri_ref, o_ref, *scratch):
    per_set = len(scratch) // 2
    for k in range(SELWIN_TILES):
        _selwin_tile(k, q_ref, bias_ref, ksa_ref, ksb_ref, vst_ref, vsd_ref, kwa_ref, kwb_ref, vwt_ref, gn_ref,
                     oc_ref, tri_ref, o_ref, *scratch[(k % 2) * per_set:(k % 2 + 1) * per_set])


def _selwin_tile(k, q_ref, bias_ref, ksa_ref, ksb_ref, vst_ref, vsd_ref, kwa_ref, kwb_ref, vwt_ref, gn_ref,
                 oc_ref, tri_ref, o_ref, acc_ref, sa_ref, sb_ref, pa_ref, pb_ref, sw_ref):
    qi = pl.program_id(2) * SELWIN_TILES + k
    rows = slice(k * TQ, (k + 1) * TQ)
    t0 = qi * TQ
    n_col = HEADS_PER_GROUP * TQ
    q_pairs = [q_ref[rows, pr * LANES:(pr + 1) * LANES] for pr in range(HEAD_PAIRS)]
    lq = jnp.concatenate(q_pairs, axis=0)
    bias = bias_ref[0, rows, :]
    ls = jnp.concatenate([jnp.concatenate([qp, bias], axis=1) for qp in q_pairs], axis=0)
    tok = t0 + jnp.bitwise_and(lax.broadcasted_iota(jnp.int32, (1, n_col), 1), TQ - 1)
    jd = t0 // TK

    def scores(j):
        k_lo = pl.multiple_of(j * TK, TK)
        s = jnp.concatenate([_dot_nt(ksa_ref[0, pl.ds(k_lo, TK), :], ls),
                             _dot_nt(ksb_ref[0, pl.ds(k_lo, TK), :], ls)], axis=1)
        return s, jnp.max(s, axis=0, keepdims=True)

    def weights(s, mx, m_old):
        m_new = jnp.maximum(m_old, mx)
        return m_new, jnp.exp2(m_old - m_new), jnp.exp2(s - m_new).astype(BF16)

    neg_inf = jnp.full((1, n_col), -jnp.inf, F32)
    acc_ref[...] = jnp.zeros(acc_ref.shape, F32)
    pb_ref[...] = jnp.zeros(pb_ref.shape, BF16)
    sa_ref[...], mx0 = scores(0)
    sb_ref[...], mx1 = scores(1)

    n_wt = W_WIN // TQ + 1
    win_tiles = []
    parts = []
    for r in range(n_wt):
        kt = qi - (n_wt - 1) + r
        kt_c = jnp.maximum(kt, 0)
        k_r = pl.multiple_of(kt_c * TQ, TQ)
        s_r = jnp.concatenate([_dot_nt(kwa_ref[0, pl.ds(k_r, TQ), :], lq),
                               _dot_nt(kwb_ref[0, pl.ds(k_r, TQ), :], lq)], axis=1)
        if r == 0:
            s_r = s_r + tri_ref[0]
        if r == n_wt - 1:
            s_r = s_r + tri_ref[1]
        else:
            s_r = jnp.where(kt >= 0, s_r, NEG_INF)
        parts.append(s_r)
        win_tiles.append(kt_c)
    sw = jnp.concatenate(parts, axis=0)
    mx_w = jnp.max(sw, axis=0, keepdims=True)
    sw_ref[...] = sw

    def pair(j, carry):
        m, mx_a, mx_b, a_prev = carry
        m, a_cur, pa_ref[...] = weights(sa_ref[...], mx_a, m)
        sa_ref[...], mx_a = scores(j + 2)
        acc_ref[...] = a_prev * acc_ref[...] + _dot(vst_ref[0, jnp.maximum(j - 1, 0)], pb_ref[...])
        m, a_next, pb_ref[...] = weights(sb_ref[...], mx_b, m)
        sb_ref[...], mx_b = scores(j + 3)
        acc_ref[...] = a_cur * acc_ref[...] + _dot(vst_ref[0, j], pa_ref[...])
        return m, mx_a, mx_b, a_next

    carry = (neg_inf, mx0, mx1, jnp.ones((1, n_col), F32))
    m, mx_a, mx_b, a_prev = lax.fori_loop(0, jd // 2, lambda i, c: pair(2 * i, c), carry)

    jf = 2 * (jd // 2)
    jd_even = ((k * TQ) // TK) % 2 == 0
    m, a_cur, pa_ref[...] = weights(sa_ref[...], mx_a, m)
    acc_ref[...] = a_prev * acc_ref[...] + _dot(vst_ref[0, jnp.maximum(jf - 1, 0)], pb_ref[...])
    if jd_even:
        o_sel = a_cur * acc_ref[...] + _dot(vst_ref[0, jf], pa_ref[...])
    else:
        m, a_next, pb_ref[...] = weights(sb_ref[...], mx_b, m)
        acc_ref[...] = a_cur * acc_ref[...] + _dot(vst_ref[0, jf], pa_ref[...])
        o_sel = a_next * acc_ref[...] + _dot(vst_ref[0, jf + 1], pb_ref[...])
    kd = pl.multiple_of(t0, TQ)
    sd = jnp.concatenate([_dot_nt(ksa_ref[0, pl.ds(kd, TQ), 0:LANES], lq),
                          _dot_nt(ksb_ref[0, pl.ds(kd, TQ), 0:LANES], lq)], axis=1)
    key_d = t0 + lax.broadcasted_iota(jnp.int32, (TQ, 1), 0)
    own = jnp.logical_and(jnp.right_shift(key_d, 6) == jnp.right_shift(tok, 6), key_d <= tok)
    sd = jnp.where(own, sd, NEG_INF)
    _, a_own, pd = weights(sd, jnp.max(sd, axis=0, keepdims=True), m)
    o_sel = a_own * o_sel + _dot(vsd_ref[0, qi], pd)
    pw = jnp.exp2(sw_ref[...] - mx_w).astype(BF16)
    o_win = _dot(vwt_ref[0, win_tiles[0]], pw[0:TQ])
    for r in range(1, n_wt):
        o_win = o_win + _dot(vwt_ref[0, win_tiles[r]], pw[r * TQ:(r + 1) * TQ])
    os_t = o_sel[:HEAD_DIM] * (1.0 / o_sel[HEAD_DIM:HEAD_DIM + 1])
    ow_t = o_win[:HEAD_DIM] * (1.0 / o_win[HEAD_DIM:HEAD_DIM + 1])

    gn_t = gn_ref[rows, :].T
    for pr in range(HEAD_PAIRS):
        halves = []
        for e in range(2):
            c0 = (e * HEAD_PAIRS + pr) * TQ
            hh = 2 * pr + e
            g_sel = gn_t[HEADS_PER_GROUP + hh:HEADS_PER_GROUP + hh + 1, :]
            g_win = gn_t[2 * HEADS_PER_GROUP + hh:2 * HEADS_PER_GROUP + hh + 1, :]
            halves.append(os_t[:, c0:c0 + TQ] * g_sel + ow_t[:, c0:c0 + TQ] * g_win)
        cols = slice(pr * LANES, (pr + 1) * LANES)
        o_ref[rows, cols] = (jnp.concatenate(halves, axis=0).T + oc_ref[rows, cols]).astype(BF16)


def _selwin_attention(q, bias, ksa, ksb, vst, vsd, kwa, kwb, vwt, gn, oc, tri, batch, seq):
    t = q.shape[0]
    rows = SELWIN_TILES * TQ
    nq = seq // rows
    n_col = HEADS_PER_GROUP * TQ
    tile_row = lambda b, g, i: (b * nq + i, g)
    grp = lambda b, g, i: (g, b, 0)
    grp4 = lambda b, g, i: (g, b, 0, 0)
    once = pl.Buffered(1)
    scratch_set = [pltpu.VMEM((V_ROWS, n_col), F32),
                   pltpu.VMEM((TK, n_col), F32), pltpu.VMEM((TK, n_col), F32),
                   pltpu.VMEM((TK, n_col), BF16), pltpu.VMEM((TK, n_col), BF16),
                   pltpu.VMEM((W_WIN + TQ, n_col), F32)]
    return pl.pallas_call(
        _selwin_body,
        grid=(batch, N_KV_GROUPS, nq),
        in_specs=[
            pl.BlockSpec((rows, 2 * LANES), tile_row),
            pl.BlockSpec((1, rows, LANES), lambda b, g, i: (g, b * nq + i, 0)),
            pl.BlockSpec((1, seq, 2 * LANES), grp, pipeline_mode=once),
            pl.BlockSpec((1, seq, 2 * LANES), grp, pipeline_mode=once),
            pl.BlockSpec((1, seq // TK, V_ROWS, TK), grp4),
            pl.BlockSpec((1, seq // TQ, V_ROWS, TQ), grp4),
            pl.BlockSpec((1, seq, LANES), grp),
            pl.BlockSpec((1, seq, LANES), grp),
            pl.BlockSpec((1, seq // TQ, V_ROWS, TQ), grp4),
            pl.BlockSpec((rows, LANES), tile_row),
            pl.BlockSpec((rows, 2 * LANES), tile_row),
            pl.BlockSpec(tri.shape, lambda b, g, i: (0, 0, 0), pipeline_mode=once),
        ],
        out_specs=pl.BlockSpec((rows, 2 * LANES), tile_row),
        out_shape=jax.ShapeDtypeStruct((t, Q_W), BF16),
        scratch_shapes=scratch_set + scratch_set,
        compiler_params=_params("parallel", "parallel", "arbitrary"),
        name="selwin_attn",
    )(q, bias, ksa, ksb, vst, vsd, kwa, kwb, vwt, gn, oc, tri)


def _merge_body(x_ref, g_ref, wg_ref, ya_ref, yb_ref, pa_ref, pb_ref, wo_ref, o_ref):
    x = x_ref[...]
    d = x.shape[1]
    h = _rms(x, g_ref[...]).astype(BF16)
    gates = jax.nn.sigmoid(_dot(h, wg_ref[...]))
    merged = gates[:, :d] * _dot(ya_ref[...], pa_ref[...]) + gates[:, d:] * _dot(yb_ref[...], pb_ref[...])
    o_ref[...] = x + _dot(merged.astype(BF16), wo_ref[...])


def _merge(x2, g, wg, ya, yb, pa, pb, wo):
    t, d = x2.shape
    const = lambda i: (0, 0)
    row = lambda i: (i, 0)
    return pl.pallas_call(
        _merge_body,
        grid=(t // TM,),
        in_specs=[
            pl.BlockSpec((TM, d), row),
            pl.BlockSpec((1, d), const),
            pl.BlockSpec((d, 2 * d), const),
            pl.BlockSpec((TM, Q_W), row),
            pl.BlockSpec((TM, SGU_W), row),
            pl.BlockSpec((Q_W, d), const),
            pl.BlockSpec((SGU_W, d), const),
            pl.BlockSpec((d, d), const),
        ],
        out_specs=pl.BlockSpec((TM, d), row),
        out_shape=jax.ShapeDtypeStruct((t, d), F32),
        compiler_params=_params("parallel"),
        name="merge",
    )(x2, g, wg, ya, yb, pa, pb, wo)


def _rot_cols(w, heads):
    d = w.shape[0]
    w4 = w.reshape(d, heads, 2, HEAD_DIM // 2)
    return jnp.concatenate([-w4[:, :, 1], w4[:, :, 0]], axis=-1).reshape(d, heads * HEAD_DIM)


def _rope_tables(pos):
    inv = ROPE_THETA ** (-jnp.arange(0, HEAD_DIM, 2, dtype=F32) / HEAD_DIM)
    ang = pos.astype(F32)[:, None] * inv[None, :]
    cos = jnp.concatenate([jnp.cos(ang)] * (2 * LANES // HEAD_DIM), axis=-1)
    sin = jnp.concatenate([jnp.sin(ang)] * (2 * LANES // HEAD_DIM), axis=-1)
    return cos, sin


def _inproj_weight(w_in):
    q = w_in[:, 0:Q_W]
    ks = w_in[:, _OFF_KS:_OFF_KS + KV_W]
    kw = w_in[:, _OFF_KW:_OFF_KW + KV_W]
    gn0 = Q_W + 6 * KV_W
    zero = jnp.zeros((w_in.shape[0], LANES - 3 * HEADS_PER_GROUP), w_in.dtype)
    gn = []
    for g in range(N_KV_GROUPS):
        w_g = w_in[:, gn0 + g * 3 * HEADS_PER_GROUP:gn0 + (g + 1) * 3 * HEADS_PER_GROUP]
        gn += [w_g.reshape(-1, HEADS_PER_GROUP, 3).transpose(0, 2, 1).reshape(-1, 3 * HEADS_PER_GROUP), zero]
    uv0 = gn0 + NSA_GATE_W
    uv = w_in[:, uv0:uv0 + 2 * SGU_W]
    w = jnp.concatenate([w_in[:, :gn0], _rot_cols(q, N_HEADS), _rot_cols(ks, N_KV_GROUPS),
                         _rot_cols(kw, N_KV_GROUPS), *gn, uv], axis=1)
    assert w.shape[1] == _IN_COLS
    return w.astype(BF16), w_in[:, uv0 + 2 * SGU_W:].astype(BF16)


def _compress_params(pe_k, pe_v, w1k, w1v, w2k, w2v):
    half = L_CMP // 2

    def pe_rows(pe):
        rep = jnp.broadcast_to(pe[:, None, :], (L_CMP, N_KV_GROUPS, HEAD_DIM))
        return (rep[:half].reshape(1, -1), rep[half:].reshape(1, -1))

    def w1_halves(w1):
        w = w1.reshape(L_CMP, HEAD_DIM, PHI_HIDDEN)
        zero = jnp.zeros((half, HEAD_DIM, PHI_HIDDEN), w.dtype)
        outs = []
        for part in (w[:half], w[half:]):
            blocks = jnp.stack([jnp.concatenate([part, zero], axis=-1),
                                jnp.concatenate([zero, part], axis=-1)], axis=1)
            outs.append(blocks.reshape(half * N_KV_GROUPS * HEAD_DIM, N_KV_GROUPS * PHI_HIDDEN).astype(BF16))
        return outs

    zero = jnp.zeros_like(w2k)
    w2k_rot = _rot_cols(w2k, 1)
    w2k_all = jnp.concatenate([w2k, zero, zero, w2k, w2k_rot, zero, zero, w2k_rot], axis=1).astype(BF16)
    pkt, pkb = pe_rows(pe_k)
    pvt, pvb = pe_rows(pe_v)
    return (pkt, pkb, pvt, pvb), (*w1_halves(w1k), *w1_halves(w1v)), w2k_all, w2v.T.astype(BF16)


def _window_triangles():
    key = np.arange(TQ)[:, None]
    tok = np.arange(HEADS_PER_GROUP * TQ)[None, :] % TQ
    upper = np.where(key > tok, 0.0, NEG_INF)
    lower = np.where(key <= tok, 0.0, NEG_INF)
    return jnp.asarray(np.stack([upper, lower]), dtype=F32)


def _sel_map_t(seq):
    n_rows = seq // CMP_STRIDE
    n_cmp = (seq - L_CMP) // CMP_STRIDE + 1
    cmp_start = np.arange(n_rows) * CMP_STRIDE
    sel_start = np.arange(LANES) * L_SEL
    ov = (np.minimum(cmp_start[None, :] + L_CMP, sel_start[:, None] + L_SEL)
          - np.maximum(cmp_start[None, :], sel_start[:, None]))
    m = np.clip(ov, 0, None).astype(np.float32) / L_CMP
    m[:, n_cmp:] = 0.0
    m[seq // L_SEL:, :] = 0.0
    return jnp.asarray(m, dtype=BF16)


def kernel(x, ffn1_norm, ffn1_w_gate_up, ffn1_w_down, mix_norm, w_in, cmp_pos_k, cmp_pos_v,
           phi_k_w1, phi_k_w2, phi_v_w1, phi_v_w2, sgu_norm, sgu_w_s, sgu_b_s, proj_a, proj_b,
           w_out, ffn2_norm, ffn2_w_gate_up, ffn2_w_down, final_norm):
    batch, seq, d = x.shape
    depth = w_in.shape[0]
    assert seq % TM == 0 and seq % TK == 0 and seq >= N_TOPK * L_SEL and seq // L_SEL <= LANES
    assert seq % (2 * TK) == 0 and seq >= W_WIN + TQ and W_WIN % TQ == 0 and seq % (SELWIN_TILES * TQ) == 0 and (SELWIN_TILES * TQ) % (2 * TK) == 0 and N_KV_GROUPS == 2 and HEADS_PER_GROUP == 4
    t = batch * seq
    n_rows = seq // CMP_STRIDE

    cos, sin = _rope_tables(jnp.arange(seq))
    cos_c, sin_c = _rope_tables(jnp.arange(n_rows) * CMP_STRIDE + (L_CMP - 1))
    selt = _sel_map_t(seq)
    tri = _window_triangles()
    row = lambda v: v.reshape(1, -1)

    x2 = x.reshape(t, d)
    for l in range(depth):
        x2 = _ffn(x2, row(ffn1_norm[l]), ffn1_w_gate_up[l].astype(BF16), ffn1_w_down[l].astype(BF16),
                  row(final_norm), False)

        w_main, w_gates = _inproj_weight(w_in[l])
        bs = jnp.repeat(sgu_b_s[l].T, SGU_HEAD, axis=1)
        (q, kc, vc, ksa, ksb, vst, vsd, kwa, kwb, vwt, gn, yb) = _inproj(
            x2, row(mix_norm[l]), w_main, cos, sin, row(sgu_norm[l]), sgu_w_s[l], bs, seq)

        pe, w1, w2k_all, w2vt = _compress_params(
            cmp_pos_k[l], cmp_pos_v[l], phi_k_w1[l], phi_v_w1[l], phi_k_w2[l], phi_v_w2[l])
        width = CMP_STRIDE * KV_W
        kca, kcb, vct = _compress(kc.reshape(t // CMP_STRIDE, width), vc.reshape(t // CMP_STRIDE, width),
                                  pe, w1, w2k_all, w2vt, cos_c, sin_c, batch)

        o_cmp, bias = _cmp_attention(q, kca, kcb, vct, gn, selt, batch, seq)
        ya = _selwin_attention(q, bias, ksa, ksb, vst, vsd, kwa, kwb, vwt, gn, o_cmp, tri, batch, seq)

        x2 = _merge(x2, row(mix_norm[l]), w_gates, ya, yb,
                    proj_a[l].astype(BF16), proj_b[l].astype(BF16), w_out[l].astype(BF16))

        x2 = _ffn(x2, row(ffn2_norm[l]), ffn2_w_gate_up[l].astype(BF16), ffn2_w_down[l].astype(BF16),
                  row(final_norm), l == depth - 1)
    return x2.reshape(batch, seq, d)
```

```python
import functools

import jax
import jax.numpy as jnp
import numpy as np
from jax import lax
from jax.experimental import pallas as pl
from jax.experimental.pallas import tpu as pltpu

HEAD_DIM = 64
N_HEADS = 8
N_KV_GROUPS = 2
HEADS_PER_GROUP = N_HEADS // N_KV_GROUPS
HEAD_PAIRS = HEADS_PER_GROUP // 2
Q_W = N_HEADS * HEAD_DIM
KV_W = N_KV_GROUPS * HEAD_DIM
NSA_GATE_W = N_HEADS * 3
L_CMP = 32
CMP_STRIDE = 16
L_SEL = 64
N_TOPK = 16
W_WIN = 512
PHI_HIDDEN = 256
SGU_GROUPS = 8
SGU_HEAD = 64
SGU_W = SGU_GROUPS * SGU_HEAD
CHUNK = 128
ROPE_THETA = 10000.0
EPS = 1e-6
NEG_INF = -1e30
FORCE = 1e9
SCALE = HEAD_DIM ** -0.5
LOG2E = float(np.log2(np.e))
V_ROWS = HEAD_DIM + 16

LANES = 128
HALF = LANES // 2
VMEM_LIMIT = 56 * 1024 * 1024

TM = 512
TQ = 256
TQC = 256
TK = 512
SELWIN_TILES = 4
FF_CHUNK = 256

F32 = jnp.float32
BF16 = jnp.bfloat16


def _dot(a, b):
    return jnp.dot(a, b, preferred_element_type=F32)


def _dot_nt(a, b):
    return lax.dot_general(a, b, (((1,), (1,)), ((), ())), preferred_element_type=F32)


def _rms(x, g):
    return x * lax.rsqrt(jnp.mean(x * x, axis=-1, keepdims=True) + EPS) * g


def _gelu_tanh(x):
    c = np.float32(np.sqrt(2.0 / np.pi))
    return 0.5 * x * (1.0 + jnp.tanh(c * (x + 0.044715 * (x * x * x))))


def _params(*sem):
    return pltpu.CompilerParams(dimension_semantics=sem, vmem_limit_bytes=VMEM_LIMIT)


def _ffn_body(x_ref, g_ref, wgu_ref, wd_ref, fg_ref, o_ref, *, d_ff, final_norm):
    x = x_ref[...]
    h = _rms(x, g_ref[...]).astype(BF16)
    acc = jnp.zeros(x.shape, F32)
    for c in range(d_ff // FF_CHUNK):
        lo = c * FF_CHUNK
        gate = _dot(h, wgu_ref[:, lo:lo + FF_CHUNK])
        up = _dot(h, wgu_ref[:, d_ff + lo:d_ff + lo + FF_CHUNK])
        act = (gate * jax.nn.sigmoid(gate) * up).astype(BF16)
        acc = acc + _dot(act, wd_ref[lo:lo + FF_CHUNK, :])
    y = x + 0.5 * acc
    if final_norm:
        y = _rms(y, fg_ref[...])
    o_ref[...] = y


def _ffn(x2, g, wgu, wd, fg, final_norm):
    t, d = x2.shape
    d_ff = wd.shape[0]
    assert t % TM == 0 and d_ff % FF_CHUNK == 0
    const = lambda i: (0, 0)
    return pl.pallas_call(
        functools.partial(_ffn_body, d_ff=d_ff, final_norm=final_norm),
        grid=(t // TM,),
        in_specs=[
            pl.BlockSpec((TM, d), lambda i: (i, 0)),
            pl.BlockSpec((1, d), const),
            pl.BlockSpec((d, 2 * d_ff), const, pipeline_mode=pl.Buffered(1)),
            pl.BlockSpec((d_ff, d), const, pipeline_mode=pl.Buffered(1)),
            pl.BlockSpec((1, d), const),
        ],
        out_specs=pl.BlockSpec((TM, d), lambda i: (i, 0)),
        out_shape=jax.ShapeDtypeStruct((t, d), F32),
        compiler_params=_params("parallel"),
        name="ffn",
    )(x2, g, wgu, wd, fg)


_OFF_Q, _OFF_KC, _OFF_VC, _OFF_KS, _OFF_VS, _OFF_KW, _OFF_VW = 0, 512, 640, 768, 896, 1024, 1152
_OFF_RQ, _OFF_RKS, _OFF_RKW, _OFF_GN, _OFF_UV = 1280, 1792, 1920, 2048, 2304
_IN_COLS = _OFF_UV + 2 * SGU_W


def _inproj_body(x_ref, g_ref, w_ref, cos_ref, sin_ref, sgn_ref, ws_ref, bs_ref,
                 q_ref, kc_ref, vc_ref, ksa_ref, ksb_ref, vst_ref, vsd_ref, kwa_ref, kwb_ref, vwt_ref,
                 gn_ref, yb_ref, *, tiles_per_seq):
    h = _rms(x_ref[...], g_ref[...]).astype(BF16)
    p = _dot(h, w_ref[...])
    cos = cos_ref[...]
    sin = sin_ref[...]
    cos4 = jnp.concatenate([cos] * (Q_W // LANES), axis=1)
    sin4 = jnp.concatenate([sin] * (Q_W // LANES), axis=1)

    q = p[:, _OFF_Q:_OFF_Q + Q_W] * cos4 + p[:, _OFF_RQ:_OFF_RQ + Q_W] * sin4
    q_ref[...] = (q * (SCALE * LOG2E)).astype(BF16)
    kc_ref[...] = p[:, _OFF_KC:_OFF_KC + KV_W]
    vc_ref[...] = p[:, _OFF_VC:_OFF_VC + KV_W]

    ksel = p[:, _OFF_KS:_OFF_KS + KV_W] * cos + p[:, _OFF_RKS:_OFF_RKS + KV_W] * sin
    kwin = p[:, _OFF_KW:_OFF_KW + KV_W] * cos + p[:, _OFF_RKW:_OFF_RKW + KV_W] * sin

    shape = ksel.shape
    lane = lax.broadcasted_iota(jnp.int32, shape, 1)
    row = lax.broadcasted_iota(jnp.int32, shape, 0)
    lo = lane < HALF
    pos = (pl.program_id(0) % tiles_per_seq) * TM + row
    onehot = jnp.where(lane == jnp.right_shift(pos, 6), 1.0, 0.0).astype(BF16)
    ksel_sw = pltpu.roll(ksel, HALF, 1)
    kwin_sw = pltpu.roll(kwin, HALF, 1)
    for g, (ks_lo, ks_hi, kw_lo, kw_hi) in enumerate(((ksel, ksel_sw, kwin, kwin_sw),
                                                      (ksel_sw, ksel, kwin_sw, kwin))):
        ksa_ref[g] = jnp.concatenate([jnp.where(lo, ks_lo, 0.0).astype(BF16), onehot], axis=1)
        ksb_ref[g] = jnp.concatenate([jnp.where(lo, 0.0, ks_hi).astype(BF16), onehot], axis=1)
        kwa_ref[g] = jnp.where(lo, kw_lo, 0.0).astype(BF16)
        kwb_ref[g] = jnp.where(lo, 0.0, kw_hi).astype(BF16)

    vsel_t = p[:, _OFF_VS:_OFF_VS + KV_W].T
    vwin_t = p[:, _OFF_VW:_OFF_VW + KV_W].T
    ones_row = jnp.where(lax.broadcasted_iota(jnp.int32, (V_ROWS - HEAD_DIM, TM), 0) == 0, 1.0, 0.0)
    for g in range(N_KV_GROUPS):
        rows = slice(g * HEAD_DIM, (g + 1) * HEAD_DIM)
        vs_aug = jnp.concatenate([vsel_t[rows, :], ones_row], axis=0).astype(BF16)
        vw_aug = jnp.concatenate([vwin_t[rows, :], ones_row], axis=0).astype(BF16)
        vst_ref[g, 0] = vs_aug
        for r in range(TM // TQ):
            vsd_ref[g, r] = vs_aug[:, r * TQ:(r + 1) * TQ]
            vwt_ref[g, r] = vw_aug[:, r * TQ:(r + 1) * TQ]

    gn_ref[...] = jax.nn.sigmoid(p[:, _OFF_GN:_OFF_GN + 2 * LANES])

    ge = _gelu_tanh(p[:, _OFF_UV:_OFF_UV + 2 * SGU_W])
    u = ge[:, :SGU_W]
    v = _rms(ge[:, SGU_W:], sgn_ref[...])
    tri = (lax.broadcasted_iota(jnp.int32, (CHUNK, CHUNK), 0)
           >= lax.broadcasted_iota(jnp.int32, (CHUNK, CHUNK), 1))
    lo_c = lax.broadcasted_iota(jnp.int32, (CHUNK, LANES), 1) < HALF
    n_chunks = TM // CHUNK
    cols = []
    for pr in range(SGU_W // LANES):
        w_even = jnp.where(tri, ws_ref[2 * pr], 0.0).astype(BF16)
        w_odd = jnp.where(tri, ws_ref[2 * pr + 1], 0.0).astype(BF16)
        rows = []
        for r in range(n_chunks):
            vp = v[r * CHUNK:(r + 1) * CHUNK, pr * LANES:(pr + 1) * LANES]
            v_even = jnp.where(lo_c, vp, 0.0).astype(BF16)
            v_odd = jnp.where(lo_c, 0.0, vp).astype(BF16)
            rows.append(_dot(w_even, v_even) + _dot(w_odd, v_odd))
        cols.append(jnp.concatenate(rows, axis=0))
    vo = jnp.concatenate(cols, axis=1)
    bias = jnp.concatenate([bs_ref[...]] * n_chunks, axis=0)
    yb_ref[...] = (u * (vo + bias)).astype(BF16)


def _inproj(x2, g, w, cos, sin, sgn, ws, bs, seq):
    t, d = x2.shape
    assert TM == TK
    tiles_per_seq = seq // TM
    const2 = lambda i: (0, 0)
    row = lambda i: (i, 0)
    grp = lambda i: (0, i, 0)
    grp4 = lambda i: (0, i, 0, 0)
    pos = lambda i: (i % tiles_per_seq, 0)
    ng = N_KV_GROUPS
    return pl.pallas_call(
        functools.partial(_inproj_body, tiles_per_seq=tiles_per_seq),
        grid=(t // TM,),
        in_specs=[
            pl.BlockSpec((TM, d), row),
            pl.BlockSpec((1, d), const2),
            pl.BlockSpec((d, _IN_COLS), const2, pipeline_mode=pl.Buffered(1)),
            pl.BlockSpec((TM, LANES), pos),
            pl.BlockSpec((TM, LANES), pos),
            pl.BlockSpec((1, SGU_W), const2),
            pl.BlockSpec((SGU_GROUPS, CHUNK, CHUNK), lambda i: (0, 0, 0)),
            pl.BlockSpec((CHUNK, SGU_W), const2),
        ],
        out_specs=[
            pl.BlockSpec((TM, Q_W), row),
            pl.BlockSpec((TM, KV_W), row),
            pl.BlockSpec((TM, KV_W), row),
            pl.BlockSpec((ng, TM, 2 * LANES), grp),
            pl.BlockSpec((ng, TM, 2 * LANES), grp),
            pl.BlockSpec((ng, 1, V_ROWS, TK), grp4),
            pl.BlockSpec((ng, TM // TQ, V_ROWS, TQ), grp4),
            pl.BlockSpec((ng, TM, LANES), grp),
            pl.BlockSpec((ng, TM, LANES), grp),
            pl.BlockSpec((ng, TM // TQ, V_ROWS, TQ), grp4),
            pl.BlockSpec((TM, 2 * LANES), row),
            pl.BlockSpec((TM, SGU_W), row),
        ],
        out_shape=[
            jax.ShapeDtypeStruct((t, Q_W), BF16),
            jax.ShapeDtypeStruct((t, KV_W), F32),
            jax.ShapeDtypeStruct((t, KV_W), F32),
            jax.ShapeDtypeStruct((ng, t, 2 * LANES), BF16),
            jax.ShapeDtypeStruct((ng, t, 2 * LANES), BF16),
            jax.ShapeDtypeStruct((ng, t // TK, V_ROWS, TK), BF16),
            jax.ShapeDtypeStruct((ng, t // TQ, V_ROWS, TQ), BF16),
            jax.ShapeDtypeStruct((ng, t, LANES), BF16),
            jax.ShapeDtypeStruct((ng, t, LANES), BF16),
            jax.ShapeDtypeStruct((ng, t // TQ, V_ROWS, TQ), BF16),
            jax.ShapeDtypeStruct((t, 2 * LANES), F32),
            jax.ShapeDtypeStruct((t, SGU_W), BF16),
        ],
        compiler_params=_params("parallel"),
        name="inproj",
    )(x2, g, w, cos, sin, sgn, ws, bs)


def _compress_body(kc_ref, vc_ref, pekt_ref, pekb_ref, pevt_ref, pevb_ref,
                   wkt_ref, wkb_ref, wvt_ref, wvb_ref, w2k_ref, w2vt_ref, cos_ref, sin_ref,
                   kca_ref, kcb_ref, vct_ref):
    n_rows = kc_ref.shape[0]

    def hidden(c_ref, pet_ref, peb_ref, wt_ref, wb_ref):
        c = c_ref[...]
        top = _dot((c + pet_ref[...]).astype(BF16), wt_ref[...])
        bot = _dot((c + peb_ref[...]).astype(BF16), wb_ref[...])
        return _gelu_tanh(top + pltpu.roll(bot, n_rows - 1, 0))

    hk = hidden(kc_ref, pekt_ref, pekb_ref, wkt_ref, wkb_ref)
    hv = hidden(vc_ref, pevt_ref, pevb_ref, wvt_ref, wvb_ref)
    cos = cos_ref[...]
    sin = sin_ref[...]
    for g in range(N_KV_GROUPS):
        hkg = hk[:, g * PHI_HIDDEN:(g + 1) * PHI_HIDDEN].astype(BF16)
        hvg = hv[:, g * PHI_HIDDEN:(g + 1) * PHI_HIDDEN].astype(BF16)
        kk = _dot(hkg, w2k_ref[...])
        kca_ref[g] = (kk[:, 0:LANES] * cos + kk[:, 2 * LANES:3 * LANES] * sin).astype(BF16)
        kcb_ref[g] = (kk[:, LANES:2 * LANES] * cos + kk[:, 3 * LANES:4 * LANES] * sin).astype(BF16)
        vct_ref[g, 0] = _dot_nt(w2vt_ref[...], hvg).astype(BF16)


def _compress(kc_rows, vc_rows, pe, w1, w2k, w2vt, cos_c, sin_c, batch):
    total, width = kc_rows.shape
    n_rows = total // batch
    const = lambda b: (0, 0)
    k_shape = jax.ShapeDtypeStruct((N_KV_GROUPS, total, LANES), BF16)
    k_spec = pl.BlockSpec((N_KV_GROUPS, n_rows, LANES), lambda b: (0, b, 0))
    pe_spec = pl.BlockSpec((1, width), const)
    w1_spec = pl.BlockSpec((width, N_KV_GROUPS * PHI_HIDDEN), const)
    return pl.pallas_call(
        _compress_body,
        grid=(batch,),
        in_specs=[
            pl.BlockSpec((n_rows, width), lambda b: (b, 0)),
            pl.BlockSpec((n_rows, width), lambda b: (b, 0)),
            pe_spec, pe_spec, pe_spec, pe_spec,
            w1_spec, w1_spec, w1_spec, w1_spec,
            pl.BlockSpec((PHI_HIDDEN, 4 * LANES), const),
            pl.BlockSpec((HEAD_DIM, PHI_HIDDEN), const),
            pl.BlockSpec((n_rows, LANES), const),
            pl.BlockSpec((n_rows, LANES), const),
        ],
        out_specs=[k_spec, k_spec,
                   pl.BlockSpec((N_KV_GROUPS, 1, HEAD_DIM, n_rows), lambda b: (0, b, 0, 0))],
        out_shape=[k_shape, k_shape,
                   jax.ShapeDtypeStruct((N_KV_GROUPS, batch, HEAD_DIM, n_rows), BF16)],
        compiler_params=_params("parallel"),
        name="compress",
    )(kc_rows, vc_rows, *pe, *w1, w2k, w2vt, cos_c, sin_c)


CMP_CLASSES = 4
CMP_TILES = 4


def _cmp_body(q_ref, kca_ref, kcb_ref, vct_ref, gn_ref, selt_ref, o_ref, bias_ref):
    n_cmp = kca_ref.shape[1]
    n_blk = selt_ref.shape[0]
    n_col = HEADS_PER_GROUP * TQC

    def run(rows_cmp, rows_blk, k):
        t0 = (pl.program_id(2) * CMP_TILES + k) * TQC
        base = k * TQC
        lq = jnp.concatenate([q_ref[base:base + TQC, pr * LANES:(pr + 1) * LANES] for pr in range(HEAD_PAIRS)],
                             axis=0)
        tok = t0 + jnp.bitwise_and(lax.broadcasted_iota(jnp.int32, (1, n_col), 1), TQC - 1)
        any_valid = jnp.where(tok >= L_CMP - 1, 1.0, 0.0)
        s = jnp.concatenate([_dot_nt(kca_ref[0, 0:rows_cmp, :], lq),
                             _dot_nt(kcb_ref[0, 0:rows_cmp, :], lq)], axis=1)
        cmp_end = lax.broadcasted_iota(jnp.int32, (rows_cmp, 1), 0) * CMP_STRIDE + (L_CMP - 1)
        s = jnp.where(cmp_end <= tok, s, NEG_INF)
        e = jnp.exp2(s - jnp.max(s, axis=0, keepdims=True))
        p = e * (any_valid / jnp.sum(e, axis=0, keepdims=True))
        oc_t = _dot(vct_ref[0, 0, :, 0:rows_cmp], p.astype(BF16))
        psum = p[:, 0:TQC]
        for hh in range(1, HEADS_PER_GROUP):
            psum = psum + p[:, hh * TQC:(hh + 1) * TQC]

        selt = selt_ref[0:rows_blk, 0:rows_cmp]
        p_hi = psum.astype(BF16)
        r1 = psum - p_hi.astype(F32)
        p_mid = r1.astype(BF16)
        p_lo = (r1 - p_mid.astype(F32)).astype(BF16)
        imp = _dot(selt, p_hi) + _dot(selt, p_mid) + _dot(selt, p_lo)

        jj = lax.broadcasted_iota(jnp.int32, (rows_blk, TQC), 0)
        cur = jnp.right_shift(t0 + lax.broadcasted_iota(jnp.int32, (rows_blk, TQC), 1), 6)
        jf = jj.astype(F32)
        taken = -3e38
        val = jnp.where(jj < cur, imp, -FORCE)
        for _ in range(N_TOPK - 1):
            best = jnp.max(val, axis=0, keepdims=True)
            first = jnp.min(jnp.where(val == best, jf, 1e6), axis=0, keepdims=True)
            val = jnp.where(jf == first, taken, val)
        bias_t = jnp.where(jnp.logical_and(val == taken, jj < cur), 0.0, NEG_INF)
        if rows_blk < n_blk:
            bias_t = jnp.concatenate([bias_t, jnp.full((n_blk - rows_blk, TQC), NEG_INF, F32)], axis=0)

        for u in range(TQC // LANES):
            cols = slice(u * LANES, (u + 1) * LANES)
            rows = slice(base + u * LANES, base + (u + 1) * LANES)
            gn_t = gn_ref[rows, :].T
            bias_ref[0, rows, :] = bias_t[:, cols].T.astype(BF16)
            for pr in range(HEAD_PAIRS):
                c_even = pr * TQC + u * LANES
                c_odd = (HEAD_PAIRS + pr) * TQC + u * LANES
                even = oc_t[:, c_even:c_even + LANES] * gn_t[2 * pr:2 * pr + 1, :]
                odd = oc_t[:, c_odd:c_odd + LANES] * gn_t[2 * pr + 1:2 * pr + 2, :]
                o_ref[rows, pr * LANES:(pr + 1) * LANES] = jnp.concatenate([even, odd], axis=0).T

    steps_per_class = pl.num_programs(2) // CMP_CLASSES
    cls = pl.program_id(2) // steps_per_class

    def run_step(c):
        for k in range(CMP_TILES):
            run(n_cmp * (c + 1) // CMP_CLASSES, n_blk * (c + 1) // CMP_CLASSES, k)

    for c in range(CMP_CLASSES):
        pl.when(cls == c)(functools.partial(run_step, c))


def _cmp_attention(q, kca, kcb, vct, gn, selt, batch, seq):
    t = q.shape[0]
    n_cmp = kca.shape[1] // batch
    rows = CMP_TILES * TQC
    nq = seq // rows
    assert nq % CMP_CLASSES == 0
    k_spec = pl.BlockSpec((1, n_cmp, LANES), lambda b, g, i: (g, b, 0))
    return pl.pallas_call(
        _cmp_body,
        grid=(batch, N_KV_GROUPS, nq),
        in_specs=[
            pl.BlockSpec((rows, 2 * LANES), lambda b, g, i: (b * nq + i, g)),
            k_spec, k_spec,
            pl.BlockSpec((1, 1, HEAD_DIM, n_cmp), lambda b, g, i: (g, b, 0, 0)),
            pl.BlockSpec((rows, LANES), lambda b, g, i: (b * nq + i, g)),
            pl.BlockSpec(selt.shape, lambda b, g, i: (0, 0)),
        ],
        out_specs=[
            pl.BlockSpec((rows, 2 * LANES), lambda b, g, i: (b * nq + i, g)),
            pl.BlockSpec((1, rows, LANES), lambda b, g, i: (g, b * nq + i, 0)),
        ],
        out_shape=[
            jax.ShapeDtypeStruct((t, Q_W), F32),
            jax.ShapeDtypeStruct((N_KV_GROUPS, t, LANES), BF16),
        ],
        compiler_params=_params("parallel", "parallel", "parallel"),
        name="cmp_attn",
    )(q, kca, kcb, vct, gn, selt)


def _selwin_body(q_ref, bias_ref, ksa_ref, ksb_ref, vst_ref, vsd_ref, kwa_ref, kwb_ref, vwt_ref, gn_ref,
                 oc_ref, tri_ref, o_ref, *scratch):
    per_set = len(scratch) // 2
    for k in range(SELWIN_TILES):
        _selwin_tile(k, q_ref, bias_ref, ksa_ref, ksb_ref, vst_ref, vsd_ref, kwa_ref, kwb_ref, vwt_ref, gn_ref,
                     oc_ref, tri_ref, o_ref, *scratch[(k % 2) * per_set:(k % 2 + 1) * per_set])


def _selwin_tile(k, q_ref, bias_ref, ksa_ref, ksb_ref, vst_ref, vsd_ref, kwa_ref, kwb_ref, vwt_ref, gn_ref,
                 oc_ref, tri_ref, o_ref, acc_ref, sa_ref, sb_ref, pa_ref, pb_ref, sw_ref):
    qi = pl.program_id(2) * SELWIN_TILES + k
    rows = slice(k * TQ, (k + 1) * TQ)
    t0 = qi * TQ
    n_col = HEADS_PER_GROUP * TQ
    q_pairs = [q_ref[rows, pr * LANES:(pr + 1) * LANES] for pr in range(HEAD_PAIRS)]
    lq = jnp.concatenate(q_pairs, axis=0)
    bias = bias_ref[0, rows, :]
    ls = jnp.concatenate([jnp.concatenate([qp, bias], axis=1) for qp in q_pairs], axis=0)
    tok = t0 + jnp.bitwise_and(lax.broadcasted_iota(jnp.int32, (1, n_col), 1), TQ - 1)
    jd = t0 // TK

    def scores(j):
        k_lo = pl.multiple_of(j * TK, TK)
        s = jnp.concatenate([_dot_nt(ksa_ref[0, pl.ds(k_lo, TK), :], ls),
                             _dot_nt(ksb_ref[0, pl.ds(k_lo, TK), :], ls)], axis=1)
        return s, jnp.max(s, axis=0, keepdims=True)

    def weights(s, mx, m_old):
        m_new = jnp.maximum(m_old, mx)
        return m_new, jnp.exp2(m_old - m_new), jnp.exp2(s - m_new).astype(BF16)

    def weights_by_chunk(s_ref, p_ref, mx, m_old):
        m_new = jnp.maximum(m_old, mx)
        for c in range(TK // LANES):
            rows = slice(c * LANES, (c + 1) * LANES)
            p_ref[rows, :] = jnp.exp2(s_ref[rows, :] - m_new).astype(BF16)
        return m_new, jnp.exp2(m_old - m_new)

    neg_inf = jnp.full((1, n_col), -jnp.inf, F32)
    acc_ref[...] = jnp.zeros(acc_ref.shape, F32)
    pb_ref[...] = jnp.zeros(pb_ref.shape, BF16)
    sa_ref[...], mx0 = scores(0)
    sb_ref[...], mx1 = scores(1)

    n_wt = W_WIN // TQ + 1
    win_tiles = []
    parts = []
    for r in range(n_wt):
        kt = qi - (n_wt - 1) + r
        kt_c = jnp.maximum(kt, 0)
        k_r = pl.multiple_of(kt_c * TQ, TQ)
        s_r = jnp.concatenate([_dot_nt(kwa_ref[0, pl.ds(k_r, TQ), :], lq),
                               _dot_nt(kwb_ref[0, pl.ds(k_r, TQ), :], lq)], axis=1)
        if r == 0:
            s_r = s_r + tri_ref[0]
        if r == n_wt - 1:
            s_r = s_r + tri_ref[1]
        else:
            s_r = jnp.where(kt >= 0, s_r, NEG_INF)
        parts.append(s_r)
        win_tiles.append(kt_c)
    sw = jnp.concatenate(parts, axis=0)
    mx_w = jnp.max(sw, axis=0, keepdims=True)
    sw_ref[...] = sw

    def pair(j, carry):
        m, mx_a, mx_b, a_prev = carry
        m, a_cur = weights_by_chunk(sa_ref, pa_ref, mx_a, m)
        sa_ref[...], mx_a = scores(j + 2)
        acc_ref[...] = a_prev * acc_ref[...] + _dot(vst_ref[0, jnp.maximum(j - 1, 0)], pb_ref[...])
        m, a_next = weights_by_chunk(sb_ref, pb_ref, mx_b, m)
        sb_ref[...], mx_b = scores(j + 3)
        acc_ref[...] = a_cur * acc_ref[...] + _dot(vst_ref[0, j], pa_ref[...])
        return m, mx_a, mx_b, a_next

    carry = (neg_inf, mx0, mx1, jnp.ones((1, n_col), F32))
    m, mx_a, mx_b, a_prev = lax.fori_loop(0, jd // 2, lambda i, c: pair(2 * i, c), carry)

    jf = 2 * (jd // 2)
    jd_even = ((k * TQ) // TK) % 2 == 0
    m, a_cur = weights_by_chunk(sa_ref, pa_ref, mx_a, m)
    acc_ref[...] = a_prev * acc_ref[...] + _dot(vst_ref[0, jnp.maximum(jf - 1, 0)], pb_ref[...])
    if jd_even:
        o_sel = a_cur * acc_ref[...] + _dot(vst_ref[0, jf], pa_ref[...])
    else:
        m, a_next = weights_by_chunk(sb_ref, pb_ref, mx_b, m)
        acc_ref[...] = a_cur * acc_ref[...] + _dot(vst_ref[0, jf], pa_ref[...])
        o_sel = a_next * acc_ref[...] + _dot(vst_ref[0, jf + 1], pb_ref[...])
    kd = pl.multiple_of(t0, TQ)
    sd = jnp.concatenate([_dot_nt(ksa_ref[0, pl.ds(kd, TQ), 0:LANES], lq),
                          _dot_nt(ksb_ref[0, pl.ds(kd, TQ), 0:LANES], lq)], axis=1)
    key_d = t0 + lax.broadcasted_iota(jnp.int32, (TQ, 1), 0)
    own = jnp.logical_and(jnp.right_shift(key_d, 6) == jnp.right_shift(tok, 6), key_d <= tok)
    sd = jnp.where(own, sd, NEG_INF)
    _, a_own, pd = weights(sd, jnp.max(sd, axis=0, keepdims=True), m)
    o_sel = a_own * o_sel + _dot(vsd_ref[0, qi], pd)
    pw = jnp.exp2(sw_ref[...] - mx_w).astype(BF16)
    o_win = _dot(vwt_ref[0, win_tiles[0]], pw[0:TQ])
    for r in range(1, n_wt):
        o_win = o_win + _dot(vwt_ref[0, win_tiles[r]], pw[r * TQ:(r + 1) * TQ])
    os_t = o_sel[:HEAD_DIM] * (1.0 / o_sel[HEAD_DIM:HEAD_DIM + 1])
    ow_t = o_win[:HEAD_DIM] * (1.0 / o_win[HEAD_DIM:HEAD_DIM + 1])

    gn_t = gn_ref[rows, :].T
    for pr in range(HEAD_PAIRS):
        halves = []
        for e in range(2):
            c0 = (e * HEAD_PAIRS + pr) * TQ
            hh = 2 * pr + e
            g_sel = gn_t[HEADS_PER_GROUP + hh:HEADS_PER_GROUP + hh + 1, :]
            g_win = gn_t[2 * HEADS_PER_GROUP + hh:2 * HEADS_PER_GROUP + hh + 1, :]
            halves.append(os_t[:, c0:c0 + TQ] * g_sel + ow_t[:, c0:c0 + TQ] * g_win)
        cols = slice(pr * LANES, (pr + 1) * LANES)
        o_ref[rows, cols] = (jnp.concatenate(halves, axis=0).T + oc_ref[rows, cols]).astype(BF16)


def _selwin_attention(q, bias, ksa, ksb, vst, vsd, kwa, kwb, vwt, gn, oc, tri, batch, seq):
    t = q.shape[0]
    rows = SELWIN_TILES * TQ
    nq = seq // rows
    n_col = HEADS_PER_GROUP * TQ
    tile_row = lambda b, g, i: (b * nq + i, g)
    grp = lambda b, g, i: (g, b, 0)
    grp4 = lambda b, g, i: (g, b, 0, 0)
    once = pl.Buffered(1)
    scratch_set = [pltpu.VMEM((V_ROWS, n_col), F32),
                   pltpu.VMEM((TK, n_col), F32), pltpu.VMEM((TK, n_col), F32),
                   pltpu.VMEM((TK, n_col), BF16), pltpu.VMEM((TK, n_col), BF16),
                   pltpu.VMEM((W_WIN + TQ, n_col), F32)]
    return pl.pallas_call(
        _selwin_body,
        grid=(batch, N_KV_GROUPS, nq),
        in_specs=[
            pl.BlockSpec((rows, 2 * LANES), tile_row),
            pl.BlockSpec((1, rows, LANES), lambda b, g, i: (g, b * nq + i, 0)),
            pl.BlockSpec((1, seq, 2 * LANES), grp, pipeline_mode=once),
            pl.BlockSpec((1, seq, 2 * LANES), grp, pipeline_mode=once),
            pl.BlockSpec((1, seq // TK, V_ROWS, TK), grp4),
            pl.BlockSpec((1, seq // TQ, V_ROWS, TQ), grp4),
            pl.BlockSpec((1, seq, LANES), grp),
            pl.BlockSpec((1, seq, LANES), grp),
            pl.BlockSpec((1, seq // TQ, V_ROWS, TQ), grp4),
            pl.BlockSpec((rows, LANES), tile_row),
            pl.BlockSpec((rows, 2 * LANES), tile_row),
            pl.BlockSpec(tri.shape, lambda b, g, i: (0, 0, 0), pipeline_mode=once),
        ],
        out_specs=pl.BlockSpec((rows, 2 * LANES), tile_row),
        out_shape=jax.ShapeDtypeStruct((t, Q_W), BF16),
        scratch_shapes=scratch_set + scratch_set,
        compiler_params=_params("parallel", "parallel", "arbitrary"),
        name="selwin_attn",
    )(q, bias, ksa, ksb, vst, vsd, kwa, kwb, vwt, gn, oc, tri)


def _merge_body(x_ref, g_ref, wg_ref, ya_ref, yb_ref, pa_ref, pb_ref, wo_ref, o_ref):
    x = x_ref[...]
    d = x.shape[1]
    h = _rms(x, g_ref[...]).astype(BF16)
    gates = jax.nn.sigmoid(_dot(h, wg_ref[...]))
    merged = gates[:, :d] * _dot(ya_ref[...], pa_ref[...]) + gates[:, d:] * _dot(yb_ref[...], pb_ref[...])
    o_ref[...] = x + _dot(merged.astype(BF16), wo_ref[...])


def _merge(x2, g, wg, ya, yb, pa, pb, wo):
    t, d = x2.shape
    const = lambda i: (0, 0)
    row = lambda i: (i, 0)
    return pl.pallas_call(
        _merge_body,
        grid=(t // TM,),
        in_specs=[
            pl.BlockSpec((TM, d), row),
            pl.BlockSpec((1, d), const),
            pl.BlockSpec((d, 2 * d), const),
            pl.BlockSpec((TM, Q_W), row),
            pl.BlockSpec((TM, SGU_W), row),
            pl.BlockSpec((Q_W, d), const),
            pl.BlockSpec((SGU_W, d), const),
            pl.BlockSpec((d, d), const),
        ],
        out_specs=pl.BlockSpec((TM, d), row),
        out_shape=jax.ShapeDtypeStruct((t, d), F32),
        compiler_params=_params("parallel"),
        name="merge",
    )(x2, g, wg, ya, yb, pa, pb, wo)


def _rot_cols(w, heads):
    d = w.shape[0]
    w4 = w.reshape(d, heads, 2, HEAD_DIM // 2)
    return jnp.concatenate([-w4[:, :, 1], w4[:, :, 0]], axis=-1).reshape(d, heads * HEAD_DIM)


def _rope_tables(pos):
    inv = ROPE_THETA ** (-jnp.arange(0, HEAD_DIM, 2, dtype=F32) / HEAD_DIM)
    ang = pos.astype(F32)[:, None] * inv[None, :]
    cos = jnp.concatenate([jnp.cos(ang)] * (2 * LANES // HEAD_DIM), axis=-1)
    sin = jnp.concatenate([jnp.sin(ang)] * (2 * LANES // HEAD_DIM), axis=-1)
    return cos, sin


def _inproj_weight(w_in):
    q = w_in[:, 0:Q_W]
    ks = w_in[:, _OFF_KS:_OFF_KS + KV_W]
    kw = w_in[:, _OFF_KW:_OFF_KW + KV_W]
    gn0 = Q_W + 6 * KV_W
    zero = jnp.zeros((w_in.shape[0], LANES - 3 * HEADS_PER_GROUP), w_in.dtype)
    gn = []
    for g in range(N_KV_GROUPS):
        w_g = w_in[:, gn0 + g * 3 * HEADS_PER_GROUP:gn0 + (g + 1) * 3 * HEADS_PER_GROUP]
        gn += [w_g.reshape(-1, HEADS_PER_GROUP, 3).transpose(0, 2, 1).reshape(-1, 3 * HEADS_PER_GROUP), zero]
    uv0 = gn0 + NSA_GATE_W
    uv = w_in[:, uv0:uv0 + 2 * SGU_W]
    w = jnp.concatenate([w_in[:, :gn0], _rot_cols(q, N_HEADS), _rot_cols(ks, N_KV_GROUPS),
                         _rot_cols(kw, N_KV_GROUPS), *gn, uv], axis=1)
    assert w.shape[1] == _IN_COLS
    return w.astype(BF16), w_in[:, uv0 + 2 * SGU_W:].astype(BF16)


def _compress_params(pe_k, pe_v, w1k, w1v, w2k, w2v):
    half = L_CMP // 2

    def pe_rows(pe):
        rep = jnp.broadcast_to(pe[:, None, :], (L_CMP, N_KV_GROUPS, HEAD_DIM))
        return (rep[:half].reshape(1, -1), rep[half:].reshape(1, -1))

    def w1_halves(w1):
        w = w1.reshape(L_CMP, HEAD_DIM, PHI_HIDDEN)
        zero = jnp.zeros((half, HEAD_DIM, PHI_HIDDEN), w.dtype)
        outs = []
        for part in (w[:half], w[half:]):
            blocks = jnp.stack([jnp.concatenate([part, zero], axis=-1),
                                jnp.concatenate([zero, part], axis=-1)], axis=1)
            outs.append(blocks.reshape(half * N_KV_GROUPS * HEAD_DIM, N_KV_GROUPS * PHI_HIDDEN).astype(BF16))
        return outs

    zero = jnp.zeros_like(w2k)
    w2k_rot = _rot_cols(w2k, 1)
    w2k_all = jnp.concatenate([w2k, zero, zero, w2k, w2k_rot, zero, zero, w2k_rot], axis=1).astype(BF16)
    pkt, pkb = pe_rows(pe_k)
    pvt, pvb = pe_rows(pe_v)
    return (pkt, pkb, pvt, pvb), (*w1_halves(w1k), *w1_halves(w1v)), w2k_all, w2v.T.astype(BF16)


def _window_triangles():
    key = np.arange(TQ)[:, None]
    tok = np.arange(HEADS_PER_GROUP * TQ)[None, :] % TQ
    upper = np.where(key > tok, 0.0, NEG_INF)
    lower = np.where(key <= tok, 0.0, NEG_INF)
    return jnp.asarray(np.stack([upper, lower]), dtype=F32)


def _sel_map_t(seq):
    n_rows = seq // CMP_STRIDE
    n_cmp = (seq - L_CMP) // CMP_STRIDE + 1
    cmp_start = np.arange(n_rows) * CMP_STRIDE
    sel_start = np.arange(LANES) * L_SEL
    ov = (np.minimum(cmp_start[None, :] + L_CMP, sel_start[:, None] + L_SEL)
          - np.maximum(cmp_start[None, :], sel_start[:, None]))
    m = np.clip(ov, 0, None).astype(np.float32) / L_CMP
    m[:, n_cmp:] = 0.0
    m[seq // L_SEL:, :] = 0.0
    return jnp.asarray(m, dtype=BF16)


def kernel(x, ffn1_norm, ffn1_w_gate_up, ffn1_w_down, mix_norm, w_in, cmp_pos_k, cmp_pos_v,
           phi_k_w1, phi_k_w2, phi_v_w1, phi_v_w2, sgu_norm, sgu_w_s, sgu_b_s, proj_a, proj_b,
           w_out, ffn2_norm, ffn2_w_gate_up, ffn2_w_down, final_norm):
    batch, seq, d = x.shape
    depth = w_in.shape[0]
    assert seq % TM == 0 and seq % TK == 0 and seq >= N_TOPK * L_SEL and seq // L_SEL <= LANES
    assert seq % (2 * TK) == 0 and seq >= W_WIN + TQ and W_WIN % TQ == 0 and seq % (SELWIN_TILES * TQ) == 0 and (SELWIN_TILES * TQ) % (2 * TK) == 0 and N_KV_GROUPS == 2 and HEADS_PER_GROUP == 4
    t = batch * seq
    n_rows = seq // CMP_STRIDE

    cos, sin = _rope_tables(jnp.arange(seq))
    cos_c, sin_c = _rope_tables(jnp.arange(n_rows) * CMP_STRIDE + (L_CMP - 1))
    selt = _sel_map_t(seq)
    tri = _window_triangles()
    row = lambda v: v.reshape(1, -1)

    x2 = x.reshape(t, d)
    for l in range(depth):
        x2 = _ffn(x2, row(ffn1_norm[l]), ffn1_w_gate_up[l].astype(BF16), ffn1_w_down[l].astype(BF16),
                  row(final_norm), False)

        w_main, w_gates = _inproj_weight(w_in[l])
        bs = jnp.repeat(sgu_b_s[l].T, SGU_HEAD, axis=1)
        (q, kc, vc, ksa, ksb, vst, vsd, kwa, kwb, vwt, gn, yb) = _inproj(
            x2, row(mix_norm[l]), w_main, cos, sin, row(sgu_norm[l]), sgu_w_s[l], bs, seq)

        pe, w1, w2k_all, w2vt = _compress_params(
            cmp_pos_k[l], cmp_pos_v[l], phi_k_w1[l], phi_v_w1[l], phi_k_w2[l], phi_v_w2[l])
        width = CMP_STRIDE * KV_W
        kca, kcb, vct = _compress(kc.reshape(t // CMP_STRIDE, width), vc.reshape(t // CMP_STRIDE, width),
                                  pe, w1, w2k_all, w2vt, cos_c, sin_c, batch)

        o_cmp, bias = _cmp_attention(q, kca, kcb, vct, gn, selt, batch, seq)
        ya = _selwin_attention(q, bias, ksa, ksb, vst, vsd, kwa, kwb, vwt, gn, o_cmp, tri, batch, seq)

        x2 = _merge(x2, row(mix_norm[l]), w_gates, ya, yb,
                    proj_a[l].astype(BF16), proj_b[l].astype(BF16), w_out[l].astype(BF16))

        x2 = _ffn(x2, row(ffn2_norm[l]), ffn2_w_gate_up[l].astype(BF16), ffn2_w_down[l].astype(BF16),
                  row(final_norm), l == depth - 1)
    return x2.reshape(batch, seq, d)
```
